```python
import jax
import jax.numpy as jnp
from jax import lax
import numpy as np

D_MODEL = 2048
BATCH = 8
SEQ = 2048
DEPTH = 1

HEAD_DIM = 128
ROPE_THETA = 10000.0
NORM_EPS = 1e-6
Q_BLOCK = 128
NEG_INF = -1e30
TINY = 1e-30

NSA_HEADS = 8
NSA_KV_GROUPS = 2
NSA_HPG = NSA_HEADS // NSA_KV_GROUPS
CMP_BLOCK = 32
CMP_STRIDE = 16
CMP_HIDDEN = 256
SEL_BLOCK = 64
SEL_TOPK = 16
SEL_Q_CHUNK = 32
FORCED_SCORE = 1e4
WINDOW = 512

FOX_HEADS = 8

PEER_HEADS = 8
PEER_N_KEYS = 128
PEER_N_EXPERTS = PEER_N_KEYS * PEER_N_KEYS
PEER_KEY_DIM = 256
PEER_HALF = PEER_KEY_DIM // 2
PEER_TOPK = 16
PEER_TOKEN_CHUNK = 128

NSA_Q_W = NSA_HEADS * HEAD_DIM
NSA_KV_W = NSA_KV_GROUPS * HEAD_DIM
NSA_GATE_W = 3 * NSA_HEADS
FOX_W = FOX_HEADS * HEAD_DIM
IN_WIDTHS = (NSA_Q_W, NSA_KV_W, NSA_KV_W, NSA_KV_W, NSA_KV_W, NSA_KV_W, NSA_KV_W, NSA_GATE_W, FOX_W, FOX_W, FOX_W, FOX_HEADS, D_MODEL, D_MODEL)
IN_TOTAL = sum(IN_WIDTHS)

kernel_name = 'nsa_fox_peer_hybrid_block'


def rms_norm(x, g):
    xf = x.astype(jnp.float32)
    y = xf * lax.rsqrt(jnp.mean(xf * xf, axis=-1, keepdims=True) + NORM_EPS)
    return (y * g.astype(jnp.float32)).astype(x.dtype)


def rope_cos_sin(pos):
    inv_freq = ROPE_THETA ** (-jnp.arange(0, HEAD_DIM, 2, dtype=jnp.float32) / HEAD_DIM)
    ang = pos.astype(jnp.float32)[:, None] * inv_freq[None, :]
    return jnp.cos(ang), jnp.sin(ang)


def apply_rope(x, cos, sin):
    half = HEAD_DIM // 2
    xf = x.astype(jnp.float32)
    x1, x2 = xf[..., :half], xf[..., half:]
    c, s = cos[None, :, None, :], sin[None, :, None, :]
    return jnp.concatenate([x1 * c - x2 * s, x2 * c + x1 * s], axis=-1).astype(x.dtype)


def masked_softmax(s, mask):
    s = jnp.where(mask, s.astype(jnp.float32), NEG_INF)
    m = jnp.max(s, axis=-1, keepdims=True)
    e = jnp.where(mask, jnp.exp(s - m), 0.0)
    return e / jnp.maximum(jnp.sum(e, axis=-1, keepdims=True), TINY)


def compress(tok, pos_emb, w1, w2):
    B, S, G, D = tok.shape
    n_chunks = S // CMP_STRIDE
    ratio = CMP_BLOCK // CMP_STRIDE
    n_blk = n_chunks - ratio + 1
    chunks = tok.reshape(B, n_chunks, CMP_STRIDE, G, D)
    blocks = jnp.concatenate([chunks[:, m:m + n_blk] for m in range(ratio)], axis=2)
    blocks = blocks + pos_emb[None, None, :, None, :]
    flat = blocks.transpose(0, 1, 3, 2, 4).reshape(B, n_blk, G, CMP_BLOCK * D)
    return jax.nn.gelu(flat @ w1) @ w2


def nsa_compressed(q, kc, vc):
    S, Nc = q.shape[1], kc.shape[1]
    s = jnp.einsum('bsghd,bngd->bghsn', q, kc) * (HEAD_DIM ** -0.5)
    t = jnp.arange(S)
    end = jnp.arange(Nc) * CMP_STRIDE + CMP_BLOCK - 1
    p = masked_softmax(s, end[None, :] <= t[:, None])
    o = jnp.einsum('bghsn,bngd->bsghd', p.astype(vc.dtype), vc)
    return o, p


def select_blocks(p_cmp):
    S, Nc = p_cmp.shape[3], p_cmp.shape[4]
    Ns = S // SEL_BLOCK
    c_start = np.arange(Nc) * CMP_STRIDE
    s_start = np.arange(Ns) * SEL_BLOCK
    overlap = (c_start[:, None] < s_start[None, :] + SEL_BLOCK) & (s_start[None, :] < c_start[:, None] + CMP_BLOCK)
    imp = jnp.einsum('bghsn,nj->bgsj', p_cmp, jnp.asarray(overlap, jnp.float32))
    t = jnp.arange(S)
    cur = t // SEL_BLOCK
    j = jnp.arange(Ns)
    forced = (j[None, :] == 0) | (j[None, :] == cur[:, None]) | (j[None, :] == cur[:, None] - 1)
    future = j[None, :] * SEL_BLOCK > t[:, None]
    score = jnp.where(forced, FORCED_SCORE, imp)
    score = jnp.where(future, NEG_INF, score)
    _, idx = lax.top_k(score, min(SEL_TOPK, Ns))
    return idx


def nsa_selected(q, k, v, idx):
    B, S, G, Hg, D = q.shape
    Ns = S // SEL_BLOCK
    n = idx.shape[-1]
    kb = k.reshape(B, Ns, SEL_BLOCK, G, D).transpose(0, 3, 1, 2, 4)
    vb = v.reshape(B, Ns, SEL_BLOCK, G, D).transpose(0, 3, 1, 2, 4)
    C = SEL_Q_CHUNK
    nC = S // C
    qc = q.reshape(B, nC, C, G, Hg, D).transpose(1, 0, 2, 3, 4, 5)
    ic = idx.reshape(B, G, nC, C, n).transpose(2, 0, 1, 3, 4)
    gather = jax.vmap(jax.vmap(lambda tab, ix: tab[ix]))
    scale = HEAD_DIM ** -0.5

    def chunk(args):
        qi, ii, c0 = args
        kg = gather(kb, ii)
        vg = gather(vb, ii)
        s = jnp.einsum('bcghd,bgcnkd->bghcnk', qi, kg) * scale
        tq = c0 + jnp.arange(C)
        kpos = ii[..., None] * SEL_BLOCK + jnp.arange(SEL_BLOCK)
        mask = (kpos <= tq[None, None, :, None, None]).reshape(B, G, 1, C, n * SEL_BLOCK)
        p = masked_softmax(s.reshape(B, G, Hg, C, n * SEL_BLOCK), mask).reshape(B, G, Hg, C, n, SEL_BLOCK)
        return jnp.einsum('bghcnk,bgcnkd->bcghd', p.astype(vg.dtype), vg)

    o = lax.map(chunk, (qc, ic, jnp.arange(nC) * C))
    return o.transpose(1, 0, 2, 3, 4, 5).reshape(B, S, G, Hg, D)


def nsa_window(q, k, v):
    B, S, G, Hg, D = q.shape
    nQ = S // Q_BLOCK
    L = Q_BLOCK + WINDOW
    kp = jnp.pad(k, ((0, 0), (WINDOW, 0), (0, 0), (0, 0)))
    vp = jnp.pad(v, ((0, 0), (WINDOW, 0), (0, 0), (0, 0)))
    qb = q.reshape(B, nQ, Q_BLOCK, G, Hg, D).transpose(1, 0, 2, 3, 4, 5)
    scale = HEAD_DIM ** -0.5

    def blk(args):
        qi, i = args
        start = i * Q_BLOCK
        ki = lax.dynamic_slice_in_dim(kp, start, L, axis=1)
        vi = lax.dynamic_slice_in_dim(vp, start, L, axis=1)
        s = jnp.einsum('bqghd,bkgd->bghqk', qi, ki) * scale
        tq = start + jnp.arange(Q_BLOCK)
        spos = start - WINDOW + jnp.arange(L)
        diff = tq[:, None] - spos[None, :]
        mask = (spos[None, :] >= 0) & (diff >= 0) & (diff < WINDOW)
        p = masked_softmax(s, mask)
        return jnp.einsum('bghqk,bkgd->bqghd', p.astype(vi.dtype), vi)

    o = lax.map(blk, (qb, jnp.arange(nQ)))
    return o.transpose(1, 0, 2, 3, 4, 5).reshape(B, S, G, Hg, D)


def fox_attention(q, k, v, log_f):
    B, S, H, D = q.shape
    nQ = S // Q_BLOCK
    c = lax.cumsum(log_f, axis=1).transpose(0, 2, 1)
    qb = q.reshape(B, nQ, Q_BLOCK, H, D).transpose(1, 0, 2, 3, 4)
    cb = c.reshape(B, H, nQ, Q_BLOCK).transpose(2, 0, 1, 3)
    scale = D ** -0.5
    kpos = jnp.arange(S)

    def blk(args):
        qi, ci, i = args
        s = jnp.einsum('bqhd,bkhd->bhqk', qi, k).astype(jnp.float32) * scale + ci[..., None] - c[:, :, None, :]
        tq = i * Q_BLOCK + jnp.arange(Q_BLOCK)
        p = masked_softmax(s, kpos[None, :] <= tq[:, None])
        return jnp.einsum('bhqk,bkhd->bqhd', p.astype(v.dtype), v)

    o = lax.map(blk, (qb, cb, jnp.arange(nQ)))
    return o.transpose(1, 0, 2, 3, 4).reshape(B, S, H, D)


def peer(x, w_query, sub_keys_1, sub_keys_2, expert_down, expert_up):
    B, S, D = x.shape
    T = B * S
    xt = x.reshape(T, D)
    q = (xt @ w_query).reshape(T, PEER_HEADS, PEER_KEY_DIM)
    s1 = jnp.einsum('thd,hkd->thk', q[..., :PEER_HALF], sub_keys_1).astype(jnp.float32)
    s2 = jnp.einsum('thd,hkd->thk', q[..., PEER_HALF:], sub_keys_2).astype(jnp.float32)
    v1, i1 = lax.top_k(s1, PEER_TOPK)
    v2, i2 = lax.top_k(s2, PEER_TOPK)
    cand = (v1[..., :, None] + v2[..., None, :]).reshape(T, PEER_HEADS, PEER_TOPK * PEER_TOPK)
    cand_idx = (i1[..., :, None] * PEER_N_KEYS + i2[..., None, :]).reshape(T, PEER_HEADS, PEER_TOPK * PEER_TOPK)
    top_s, pos = lax.top_k(cand, PEER_TOPK)
    eidx = jnp.take_along_axis(cand_idx, pos, axis=-1)
    g = jax.nn.softmax(top_s, axis=-1)
    C = PEER_TOKEN_CHUNK
    nC = T // C
    HK = PEER_HEADS * PEER_TOPK

    def chunk(args):
        xi, ei, gi = args
        u = expert_down[ei]
        a = jax.nn.gelu(jnp.einsum('cd,ced->ce', xi, u)).astype(jnp.float32)
        w = (a * gi).astype(xi.dtype)
        return jnp.einsum('ce,ced->cd', w, expert_up[ei])

    out = lax.map(chunk, (xt.reshape(nC, C, D), eidx.reshape(nC, C, HK), g.reshape(nC, C, HK)))
    return out.reshape(B, S, D)


def setup_inputs(seed: int = 0) -> dict:
    key = jax.random.key(seed)
    ks = jax.random.split(key, 24)
    f32 = jnp.float32

    def nrm(k, shape, scale):
        return jax.random.normal(k, shape, f32) * scale

    return {
        'x': nrm(ks[0], (BATCH, SEQ, D_MODEL), 1.0),
        'norm_mix_gain': 1.0 + nrm(ks[1], (DEPTH, D_MODEL), 0.05),
        'w_in': nrm(ks[2], (DEPTH, D_MODEL, IN_TOTAL), D_MODEL ** -0.5),
        'nsa_gate_bias': nrm(ks[3], (DEPTH, NSA_GATE_W), 0.1),
        'fox_forget_bias': 3.0 + nrm(ks[4], (DEPTH, FOX_HEADS), 0.5),
        'merge_gate_bias': nrm(ks[5], (DEPTH, 2 * D_MODEL), 0.1),
        'k_cmp_pos': nrm(ks[6], (DEPTH, CMP_BLOCK, HEAD_DIM), 0.1),
        'k_cmp_w1': nrm(ks[7], (DEPTH, CMP_BLOCK * HEAD_DIM, CMP_HIDDEN), (CMP_BLOCK * HEAD_DIM) ** -0.5),
        'k_cmp_w2': nrm(ks[8], (DEPTH, CMP_HIDDEN, HEAD_DIM), CMP_HIDDEN ** -0.5),
        'v_cmp_pos': nrm(ks[9], (DEPTH, CMP_BLOCK, HEAD_DIM), 0.1),
        'v_cmp_w1': nrm(ks[10], (DEPTH, CMP_BLOCK * HEAD_DIM, CMP_HIDDEN), (CMP_BLOCK * HEAD_DIM) ** -0.5),
        'v_cmp_w2': nrm(ks[11], (DEPTH, CMP_HIDDEN, HEAD_DIM), CMP_HIDDEN ** -0.5),
        'w_up_nsa': nrm(ks[12], (DEPTH, NSA_Q_W, D_MODEL), NSA_Q_W ** -0.5),
        'w_up_fox': nrm(ks[13], (DEPTH, FOX_W, D_MODEL), FOX_W ** -0.5),
        'w_out': nrm(ks[14], (DEPTH, D_MODEL, D_MODEL), D_MODEL ** -0.5),
        'norm_ffn_gain': 1.0 + nrm(ks[15], (DEPTH, D_MODEL), 0.05),
        'peer_w_query': nrm(ks[16], (DEPTH, D_MODEL, PEER_HEADS * PEER_KEY_DIM), D_MODEL ** -0.5),
        'peer_sub_keys_1': nrm(ks[17], (DEPTH, PEER_HEADS, PEER_N_KEYS, PEER_HALF), PEER_HALF ** -0.5),
        'peer_sub_keys_2': nrm(ks[18], (DEPTH, PEER_HEADS, PEER_N_KEYS, PEER_HALF), PEER_HALF ** -0.5),
        'peer_expert_down': nrm(ks[19], (DEPTH, PEER_N_EXPERTS, D_MODEL), D_MODEL ** -0.5),
        'peer_expert_up': nrm(ks[20], (DEPTH, PEER_N_EXPERTS, D_MODEL), 0.5),
        'norm_final_gain': 1.0 + nrm(ks[21], (D_MODEL,), 0.05),
    }


def reference(x, norm_mix_gain, w_in, nsa_gate_bias, fox_forget_bias, merge_gate_bias, k_cmp_pos, k_cmp_w1, k_cmp_w2, v_cmp_pos, v_cmp_w1, v_cmp_w2, w_up_nsa, w_up_fox, w_out, norm_ffn_gain, peer_w_query, peer_sub_keys_1, peer_sub_keys_2, peer_expert_down, peer_expert_up, norm_final_gain):
    B, S, _ = x.shape
    cos_t, sin_t = rope_cos_sin(jnp.arange(S))
    n_cmp = S // CMP_STRIDE - CMP_BLOCK // CMP_STRIDE + 1
    cos_c, sin_c = rope_cos_sin(jnp.arange(n_cmp) * CMP_STRIDE + CMP_BLOCK - 1)
    split_at = np.cumsum(np.array(IN_WIDTHS))[:-1].tolist()

    def grp(a):
        return a.reshape(B, S, NSA_KV_GROUPS, HEAD_DIM)

    for l in range(DEPTH):
        h = rms_norm(x, norm_mix_gain[l])
        proj = jnp.einsum('bsd,de->bse', h, w_in[l])
        (q_n, k_c, v_c, k_s, v_s, k_w, v_w, g_n, q_f, k_f, v_f, f_f, g_a, g_b) = jnp.split(proj, split_at, axis=-1)

        q_n = apply_rope(q_n.reshape(B, S, NSA_HEADS, HEAD_DIM), cos_t, sin_t).reshape(B, S, NSA_KV_GROUPS, NSA_HPG, HEAD_DIM)
        kc = apply_rope(compress(grp(k_c), k_cmp_pos[l], k_cmp_w1[l], k_cmp_w2[l]), cos_c, sin_c)
        vc = compress(grp(v_c), v_cmp_pos[l], v_cmp_w1[l], v_cmp_w2[l])
        o_cmp, p_cmp = nsa_compressed(q_n, kc, vc)
        idx = select_blocks(p_cmp)
        o_sel = nsa_selected(q_n, apply_rope(grp(k_s), cos_t, sin_t), grp(v_s), idx)
        o_win = nsa_window(q_n, apply_rope(grp(k_w), cos_t, sin_t), grp(v_w))
        gates = jax.nn.sigmoid(g_n + nsa_gate_bias[l]).reshape(B, S, NSA_KV_GROUPS, NSA_HPG, 3)
        o_nsa = (gates[..., 0:1] * o_cmp + gates[..., 1:2] * o_sel + gates[..., 2:3] * o_win).reshape(B, S, NSA_Q_W)

        log_f = jax.nn.log_sigmoid((f_f + fox_forget_bias[l]).astype(jnp.float32))
        o_fox = fox_attention(q_f.reshape(B, S, FOX_HEADS, HEAD_DIM), k_f.reshape(B, S, FOX_HEADS, HEAD_DIM), v_f.reshape(B, S, FOX_HEADS, HEAD_DIM), log_f).reshape(B, S, FOX_W)

        merged = (jax.nn.sigmoid(g_a + merge_gate_bias[l, :D_MODEL]) * (o_nsa @ w_up_nsa[l])
                  + jax.nn.sigmoid(g_b + merge_gate_bias[l, D_MODEL:]) * (o_fox @ w_up_fox[l]))
        x = x + merged @ w_out[l]

        x = x + peer(rms_norm(x, norm_ffn_gain[l]), peer_w_query[l], peer_sub_keys_1[l], peer_sub_keys_2[l], peer_expert_down[l], peer_expert_up[l])
    return rms_norm(x, norm_final_gain)
```

```python
import functools
import math

import jax
import jax.numpy as jnp
import numpy as np
from jax import lax
from jax.experimental import pallas as pl
from jax.experimental.pallas import tpu as pltpu

F32 = jnp.float32
BF16 = jnp.bfloat16
I32 = jnp.int32

LANES = 128
HEAD_DIM = 128
ROPE_THETA = 10000.0
NORM_EPS = 1e-6
NEG_INF = -1e30
TINY = 1e-30
PAD_SCORE = -3e38

NSA_HEADS = 8
NSA_GROUPS = 2
NSA_HPG = NSA_HEADS // NSA_GROUPS
CMP_BLOCK = 32
CMP_STRIDE = 16
SEL_BLOCK = 64
SEL_TOPK = 16
FORCED_SCORE = 1e4
WINDOW = 512
FOX_HEADS = 8

PEER_HEADS = 8
PEER_N_KEYS = 128
PEER_HALF = 128
PEER_TOPK = 16
PEER_SLOTS = PEER_HEADS * PEER_TOPK
PEER_PARTS = 4

VMEM_LIMIT_BYTES = 56 * 1024 * 1024


def _cparams(sem):
    return pltpu.CompilerParams(dimension_semantics=sem, vmem_limit_bytes=VMEM_LIMIT_BYTES)


def _split3(x):
    hi = x.astype(BF16)
    r = x - hi.astype(F32)
    mid = r.astype(BF16)
    lo = (r - mid.astype(F32)).astype(BF16)
    return hi, mid, lo


def _dot(a, b):
    return jnp.dot(a, b, preferred_element_type=F32)


def _dot_nt(a, b):
    return lax.dot_general(a, b, (((1,), (1,)), ((), ())), preferred_element_type=F32)


def _dot3_right(x, m01):
    hi, mid, lo = _split3(x)
    return _dot(hi, m01) + _dot(mid, m01) + _dot(lo, m01)


def _dot3_left(m01, x):
    hi, mid, lo = _split3(x)
    return _dot(m01, hi) + _dot(m01, mid) + _dot(m01, lo)


def _masked_softmax(s, mask):
    s = jnp.where(mask, s, NEG_INF)
    m = jnp.max(s, axis=-1, keepdims=True)
    e = jnp.where(mask, jnp.exp(s - m), 0.0)
    return e / jnp.maximum(jnp.sum(e, axis=-1, keepdims=True), TINY)


def _rope_tile(x, cos_full, sin_signed):
    return x * cos_full + pltpu.roll(x, HEAD_DIM // 2, axis=1) * sin_signed


def _rmsnorm_kernel(x_ref, g_ref, *o_refs):
    x = x_ref[...]
    y = x * lax.rsqrt(jnp.mean(x * x, axis=-1, keepdims=True) + NORM_EPS) * g_ref[...]
    for o_ref in o_refs:
        o_ref[...] = y.astype(o_ref.dtype)


def _rmsnorm(x2, gain, out_dtypes, tm=512):
    T, D = x2.shape
    outs = pl.pallas_call(
        _rmsnorm_kernel,
        grid=(T // tm,),
        in_specs=[pl.BlockSpec((tm, D), lambda i: (i, 0)), pl.BlockSpec((1, D), lambda i: (0, 0))],
        out_specs=[pl.BlockSpec((tm, D), lambda i: (i, 0)) for _ in out_dtypes],
        out_shape=[jax.ShapeDtypeStruct((T, D), dt) for dt in out_dtypes],
        compiler_params=_cparams(("parallel",)),
        name="rmsnorm",
    )(x2, gain.reshape(1, D).astype(F32))
    return outs


def _mm_kernel(a_ref, b_ref, *rest, epilogue):
    o_ref = rest[-1]
    acc = _dot(a_ref[...], b_ref[...])
    if epilogue == "rope":
        cos_ref, sin_ref = rest[0], rest[1]
        c, s = cos_ref[...], sin_ref[...]
        for j in range(acc.shape[1] // HEAD_DIM):
            sl = slice(j * HEAD_DIM, (j + 1) * HEAD_DIM)
            o_ref[:, sl] = _rope_tile(acc[:, sl], c, s).astype(o_ref.dtype)
    elif epilogue == "sigmoid_bias":
        o_ref[...] = jax.nn.sigmoid(acc + rest[0][...]).astype(o_ref.dtype)
    elif epilogue == "residual":
        o_ref[...] = (rest[0][...] + acc).astype(o_ref.dtype)
    else:
        o_ref[...] = acc.astype(o_ref.dtype)


def _matmul(a, b, out_dtype, epilogue="none", extras=(), tm=1024, tn=512, seq=None, name="matmul"):
    M, K = a.shape
    _, N = b.shape
    tn = min(tn, N)
    tm = min(tm, M)
    in_specs = [pl.BlockSpec((tm, K), lambda i, j: (i, 0)), pl.BlockSpec((K, tn), lambda i, j: (0, j))]
    if epilogue == "rope":
        nrep = seq // tm
        in_specs += [pl.BlockSpec((tm, HEAD_DIM), lambda i, j: (i % nrep, 0))] * 2
    elif epilogue == "sigmoid_bias":
        in_specs += [pl.BlockSpec((1, tn), lambda i, j: (0, j))]
    elif epilogue == "residual":
        in_specs += [pl.BlockSpec((tm, tn), lambda i, j: (i, j))]
    return pl.pallas_call(
        functools.partial(_mm_kernel, epilogue=epilogue),
        grid=(M // tm, N // tn),
        in_specs=in_specs,
        out_specs=pl.BlockSpec((tm, tn), lambda i, j: (i, j)),
        out_shape=jax.ShapeDtypeStruct((M, N), out_dtype),
        compiler_params=_cparams(("parallel", "parallel")),
        name=name,
    )(a, b, *extras)


def _compress_kernel(ch_ref, pos_ref, w1t_ref, w1b_ref, w2_ref, cos_ref, sin_ref, o_ref, *, rope):
    ch = ch_ref[...].astype(F32)
    a_top = (ch + pos_ref[0:1, :]).astype(BF16)
    a_bot = (ch + pos_ref[1:2, :]).astype(BF16)
    y_top = _dot(a_top, w1t_ref[...])
    y_bot = _dot(a_bot, w1b_ref[...])
    n = y_bot.shape[0]
    hidden = y_top + pltpu.roll(y_bot, n - 1, axis=0)
    out = _dot(jax.nn.gelu(hidden).astype(BF16), w2_ref[...])
    if rope:
        out = _rope_tile(out, cos_ref[...], sin_ref[...])
    row = lax.broadcasted_iota(I32, out.shape, 0)
    o_ref[...] = jnp.where(row < n - 1, out, 0.0).astype(o_ref.dtype)


def _compress(chunks, pos, w1, w2, cos_c, sin_c, rope):
    BG, NC, CK = chunks.shape
    hid = w1.shape[1]
    pos2 = pos.reshape(2, CK).astype(F32)
    return pl.pallas_call(
        functools.partial(_compress_kernel, rope=rope),
        grid=(BG,),
        in_specs=[
            pl.BlockSpec((None, NC, CK), lambda i: (i, 0, 0)),
            pl.BlockSpec((2, CK), lambda i: (0, 0)),
            pl.BlockSpec((CK, hid), lambda i: (0, 0)),
            pl.BlockSpec((CK, hid), lambda i: (1, 0)),
            pl.BlockSpec((hid, HEAD_DIM), lambda i: (0, 0)),
            pl.BlockSpec((NC, HEAD_DIM), lambda i: (0, 0)),
            pl.BlockSpec((NC, HEAD_DIM), lambda i: (0, 0)),
        ],
        out_specs=pl.BlockSpec((None, NC, HEAD_DIM), lambda i: (i, 0, 0)),
        out_shape=jax.ShapeDtypeStruct((BG, NC, HEAD_DIM), BF16),
        compiler_params=_cparams(("parallel",)),
        name="compress",
    )(chunks, pos2, w1.astype(BF16), w1.astype(BF16), w2.astype(BF16), cos_c, sin_c)


def _nsa_kernel(q_ref, ks_ref, kw_ref, vs_ref, vw_ref, kc_ref, vc_ref, gate_ref, gbias_ref, ovl_ref, exp_ref,
                o_ref, *, tq, seq):
    i = pl.program_id(1)
    t0 = i * tq
    scale = HEAD_DIM ** -0.5
    n_sel = seq // SEL_BLOCK
    n_cmp = seq // CMP_STRIDE - CMP_BLOCK // CMP_STRIDE + 1
    wlen = tq + WINDOW

    t_col = t0 + lax.broadcasted_iota(I32, (tq, 1), 0)
    lane = lax.broadcasted_iota(I32, (tq, LANES), 1)
    gate = jax.nn.sigmoid(gate_ref[...] + gbias_ref[...])

    cmp_mask = (lane * CMP_STRIDE + (CMP_BLOCK - 1) <= t_col) & (lane < n_cmp)
    key_all = lax.broadcasted_iota(I32, (tq, seq), 1)
    causal = key_all <= t_col
    wstart = pl.multiple_of(jnp.maximum(t0 - WINDOW, 0), tq)
    key_w = wstart + lax.broadcasted_iota(I32, (tq, wlen), 1)
    win_mask = (key_w <= t_col) & (t_col - key_w < WINDOW)

    cur = t_col >> 6
    forced = (lane == 0) | (lane == cur) | (lane == cur - 1)
    future = lane * SEL_BLOCK > t_col

    for g in range(NSA_GROUPS):
        gs = slice(g * HEAD_DIM, (g + 1) * HEAD_DIM)
        kc = kc_ref[g]
        vc = vc_ref[g]
        o_cmp = []
        psum = jnp.zeros((tq, LANES), F32)
        for h in range(NSA_HPG):
            hs = slice((g * NSA_HPG + h) * HEAD_DIM, (g * NSA_HPG + h + 1) * HEAD_DIM)
            p = _masked_softmax(_dot_nt(q_ref[:, hs], kc) * scale, cmp_mask)
            psum = psum + p
            o_cmp.append(_dot(p.astype(BF16), vc))
        imp = _dot3_right(psum, ovl_ref[...])
        score = jnp.where(forced, FORCED_SCORE, imp)
        score = jnp.where(future, NEG_INF, score)
        score = jnp.where(lane < n_sel, score, PAD_SCORE)
        rank = jnp.zeros((tq, LANES), I32)
        for k in range(n_sel):
            ck = score[:, k:k + 1]
            beats = (ck > score) | ((ck == score) & (lane > k))
            rank = rank + beats.astype(I32)
        sel = ((rank < min(SEL_TOPK, n_sel)) & (lane < n_sel)).astype(BF16)
        sel_keys = _dot(sel, exp_ref[...]) > 0.5
        sel_mask = sel_keys & causal

        ks = ks_ref[:, gs]
        vs = vs_ref[:, gs]
        kw = kw_ref[pl.ds(wstart, wlen), gs]
        vw = vw_ref[pl.ds(wstart, wlen), gs]
        for h in range(NSA_HPG):
            hh = g * NSA_HPG + h
            hs = slice(hh * HEAD_DIM, (hh + 1) * HEAD_DIM)
            q = q_ref[:, hs]
            p = _masked_softmax(_dot_nt(q, ks) * scale, sel_mask)
            o_sel = _dot(p.astype(BF16), vs)
            p = _masked_softmax(_dot_nt(q, kw) * scale, win_mask)
            o_win = _dot(p.astype(BF16), vw)
            gl = hh * 3
            o = (gate[:, gl:gl + 1] * o_cmp[h] + gate[:, gl + 1:gl + 2] * o_sel + gate[:, gl + 2:gl + 3] * o_win)
            o_ref[:, hs] = o.astype(o_ref.dtype)


def _nsa(rope_proj, plain_proj, kc, vc, small_proj, gate_bias, B, S, tq=256):
    n_sel = S // SEL_BLOCK
    n_cmp = S // CMP_STRIDE - CMP_BLOCK // CMP_STRIDE + 1
    c_start = np.arange(LANES) * CMP_STRIDE
    s_start = np.arange(LANES) * SEL_BLOCK
    ovl = (c_start[:, None] < s_start[None, :] + SEL_BLOCK) & (s_start[None, :] < c_start[:, None] + CMP_BLOCK)
    ovl &= (np.arange(LANES)[:, None] < n_cmp) & (np.arange(LANES)[None, :] < n_sel)
    expand = (np.arange(LANES)[:, None] == (np.arange(S)[None, :] // SEL_BLOCK))
    rp = rope_proj.reshape(B, S, -1)
    pp = plain_proj.reshape(B, S, -1)
    sm = small_proj.reshape(B, S, LANES)
    gw = NSA_GROUPS * HEAD_DIM
    qw = NSA_HEADS * HEAD_DIM
    return pl.pallas_call(
        functools.partial(_nsa_kernel, tq=tq, seq=S),
        grid=(B, S // tq),
        in_specs=[
            pl.BlockSpec((None, tq, qw), lambda b, i: (b, i, 0)),
            pl.BlockSpec((None, S, gw), lambda b, i: (b, 0, qw // gw)),
            pl.BlockSpec((None, S, gw), lambda b, i: (b, 0, qw // gw + 1)),
            pl.BlockSpec((None, S, gw), lambda b, i: (b, 0, 2)),
            pl.BlockSpec((None, S, gw), lambda b, i: (b, 0, 3)),
            pl.BlockSpec((None, NSA_GROUPS, LANES, HEAD_DIM), lambda b, i: (b, 0, 0, 0)),
            pl.BlockSpec((None, NSA_GROUPS, LANES, HEAD_DIM), lambda b, i: (b, 0, 0, 0)),
            pl.BlockSpec((None, tq, LANES), lambda b, i: (b, i, 0)),
            pl.BlockSpec((1, LANES), lambda b, i: (0, 0)),
            pl.BlockSpec((LANES, LANES), lambda b, i: (0, 0)),
            pl.BlockSpec((LANES, S), lambda b, i: (0, 0)),
        ],
        out_specs=pl.BlockSpec((None, tq, qw), lambda b, i: (b, i, 0)),
        out_shape=jax.ShapeDtypeStruct((B, S, qw), BF16),
        compiler_params=_cparams(("parallel", "parallel")),
        name="nsa",
    )(rp, rp, rp, pp, pp, kc, vc, sm, gate_bias, jnp.asarray(ovl, BF16), jnp.asarray(expand, BF16))


def _forget_cumsum_kernel(sm_ref, fbias_ref, tri_ref, ccol_ref, crow_ref, *, seq, lane0):
    x = sm_ref[...] + fbias_ref[...]
    lane = lax.broadcasted_iota(I32, x.shape, 1)
    log_f = jnp.minimum(x, 0.0) - jnp.log1p(jnp.exp(-jnp.abs(x)))
    log_f = jnp.where((lane >= lane0) & (lane < lane0 + FOX_HEADS), log_f, 0.0)
    carry = jnp.zeros((1, LANES), F32)
    blk = tri_ref.shape[0]
    for r in range(seq // blk):
        c = _dot3_left(tri_ref[...], log_f[r * blk:(r + 1) * blk, :]) + carry
        ccol_ref[r * blk:(r + 1) * blk, :] = c
        carry = c[blk - 1:blk, :]
    crow_ref[...] = ccol_ref[...].T[lane0:lane0 + FOX_HEADS, :]


def _forget_cumsum(small_proj, fbias, B, S, lane0):
    tri = np.tril(np.ones((LANES, LANES), np.float32))
    return pl.pallas_call(
        functools.partial(_forget_cumsum_kernel, seq=S, lane0=lane0),
        grid=(B,),
        in_specs=[
            pl.BlockSpec((None, S, LANES), lambda b: (b, 0, 0)),
            pl.BlockSpec((1, LANES), lambda b: (0, 0)),
            pl.BlockSpec((LANES, LANES), lambda b: (0, 0)),
        ],
        out_specs=[
            pl.BlockSpec((None, S, LANES), lambda b: (b, 0, 0)),
            pl.BlockSpec((None, FOX_HEADS, S), lambda b: (b, 0, 0)),
        ],
        out_shape=[jax.ShapeDtypeStruct((B, S, LANES), F32), jax.ShapeDtypeStruct((B, FOX_HEADS, S), F32)],
        compiler_params=_cparams(("parallel",)),
        name="forget_cumsum",
    )(small_proj.reshape(B, S, LANES), fbias, jnp.asarray(tri, BF16))


def _fox_kernel(q_ref, k_ref, v_ref, ccol_ref, crow_ref, o_ref, *, tq, seq, lane0):
    i = pl.program_id(1)
    t_col = i * tq + lax.broadcasted_iota(I32, (tq, 1), 0)
    causal = lax.broadcasted_iota(I32, (tq, seq), 1) <= t_col
    scale = HEAD_DIM ** -0.5
    for h in range(FOX_HEADS):
        hs = slice(h * HEAD_DIM, (h + 1) * HEAD_DIM)
        s = _dot_nt(q_ref[:, hs], k_ref[:, hs]) * scale
        s = s + ccol_ref[:, lane0 + h:lane0 + h + 1] - crow_ref[h:h + 1, :]
        p = _masked_softmax(s, causal)
        o_ref[:, hs] = _dot(p.astype(BF16), v_ref[:, hs]).astype(o_ref.dtype)


def _fox(plain_proj, ccol, crow, B, S, lane0, tq=256):
    pp = plain_proj.reshape(B, S, -1)
    fw = FOX_HEADS * HEAD_DIM
    return pl.pallas_call(
        functools.partial(_fox_kernel, tq=tq, seq=S, lane0=lane0),
        grid=(B, S // tq),
        in_specs=[
            pl.BlockSpec((None, tq, fw), lambda b, i: (b, i, 1)),
            pl.BlockSpec((None, S, fw), lambda b, i: (b, 0, 2)),
            pl.BlockSpec((None, S, fw), lambda b, i: (b, 0, 3)),
            pl.BlockSpec((None, tq, LANES), lambda b, i: (b, i, 0)),
            pl.BlockSpec((None, FOX_HEADS, S), lambda b, i: (b, 0, 0)),
        ],
        out_specs=pl.BlockSpec((None, tq, fw), lambda b, i: (b, i, 0)),
        out_shape=jax.ShapeDtypeStruct((B, S, fw), BF16),
        compiler_params=_cparams(("parallel", "parallel")),
        name="fox",
    )(pp, pp, pp, ccol, crow)


def _merge_kernel(on_ref, of_ref, wn_ref, wf_ref, ga_ref, gb_ref, o_ref):
    a = _dot(on_ref[...], wn_ref[...])
    b = _dot(of_ref[...], wf_ref[...])
    o_ref[...] = (ga_ref[...].astype(F32) * a + gb_ref[...].astype(F32) * b).astype(o_ref.dtype)


def _merge(o_nsa, o_fox, w_up_nsa, w_up_fox, gates, d_model, tm=1024, tn=512):
    T, K = o_nsa.shape
    nb = d_model // tn
    return pl.pallas_call(
        _merge_kernel,
        grid=(T // tm, nb),
        in_specs=[
            pl.BlockSpec((tm, K), lambda i, j: (i, 0)),
            pl.BlockSpec((tm, K), lambda i, j: (i, 0)),
            pl.BlockSpec((K, tn), lambda i, j: (0, j)),
            pl.BlockSpec((K, tn), lambda i, j: (0, j)),
            pl.BlockSpec((tm, tn), lambda i, j: (i, j)),
            pl.BlockSpec((tm, tn), lambda i, j: (i, j + nb)),
        ],
        out_specs=pl.BlockSpec((tm, tn), lambda i, j: (i, j)),
        out_shape=jax.ShapeDtypeStruct((T, d_model), BF16),
        compiler_params=_cparams(("parallel", "parallel")),
        name="merge",
    )(o_nsa, o_fox, w_up_nsa, w_up_fox, gates, gates)


def _rope_tables(pos):
    inv_freq = ROPE_THETA ** (-jnp.arange(0, HEAD_DIM, 2, dtype=F32) / HEAD_DIM)
    ang = pos.astype(F32)[:, None] * inv_freq[None, :]
    c, s = jnp.cos(ang), jnp.sin(ang)
    return jnp.concatenate([c, c], axis=-1), jnp.concatenate([-s, s], axis=-1)


def _mixer(x2, B, S, norm_gain, w_in, nsa_gate_bias, fox_forget_bias, merge_gate_bias, k_cmp_pos, k_cmp_w1, k_cmp_w2,
           v_cmp_pos, v_cmp_w1, v_cmp_w2, w_up_nsa, w_up_fox, w_out):
    T, D = x2.shape
    qw = NSA_HEADS * HEAD_DIM
    gw = NSA_GROUPS * HEAD_DIM
    fw = FOX_HEADS * HEAD_DIM
    ngate = 3 * NSA_HEADS
    widths = (qw, gw, gw, gw, gw, gw, gw, ngate, fw, fw, fw, FOX_HEADS, D, D)
    offs = np.concatenate([[0], np.cumsum(widths)])
    (w_qn, w_kc, w_vc, w_ks, w_vs, w_kw, w_vw, w_gn, w_qf, w_kf, w_vf, w_ff, w_ga, w_gb) = [
        w_in[:, offs[j]:offs[j + 1]] for j in range(len(widths))]
    w_rope = jnp.concatenate([w_qn, w_ks, w_kw], axis=1).astype(BF16)
    w_plain = jnp.concatenate([w_kc, w_vc, w_vs, w_vw, w_qf, w_kf, w_vf], axis=1).astype(BF16)
    w_gate = jnp.concatenate([w_ga, w_gb], axis=1).astype(BF16)
    pad = LANES - ngate - FOX_HEADS
    w_small = jnp.concatenate([w_gn, w_ff, jnp.zeros((D, pad), w_in.dtype)], axis=1).astype(BF16)
    gate_bias = jnp.concatenate([nsa_gate_bias, jnp.zeros((LANES - ngate,), F32)]).reshape(1, LANES)
    fbias = jnp.concatenate([jnp.zeros((ngate,), F32), fox_forget_bias, jnp.zeros((pad,), F32)]).reshape(1, LANES)

    (h,) = _rmsnorm(x2, norm_gain, (BF16,))
    cos_t, sin_t = _rope_tables(jnp.arange(S))
    rope_proj = _matmul(h, w_rope, BF16, "rope", (cos_t, sin_t), seq=S, name="inproj_rope")
    plain_proj = _matmul(h, w_plain, BF16, name="inproj_plain")
    gates = _matmul(h, w_gate, BF16, "sigmoid_bias", (merge_gate_bias.reshape(1, 2 * D),), name="inproj_gate")
    small_proj = _matmul(h, w_small, F32, name="inproj_small")

    n_chunks = S // CMP_STRIDE
    ck = CMP_STRIDE * HEAD_DIM

    def chunks(col0):
        c = plain_proj[:, col0:col0 + gw].reshape(B, S, NSA_GROUPS, HEAD_DIM).transpose(0, 2, 1, 3)
        return c.reshape(B * NSA_GROUPS, n_chunks, ck)

    cos_c, sin_c = _rope_tables(jnp.arange(n_chunks) * CMP_STRIDE + CMP_BLOCK - 1)
    kc = _compress(chunks(0), k_cmp_pos, k_cmp_w1, k_cmp_w2, cos_c, sin_c, True)
    vc = _compress(chunks(gw), v_cmp_pos, v_cmp_w1, v_cmp_w2, cos_c, sin_c, False)
    kc = kc.reshape(B, NSA_GROUPS, n_chunks, HEAD_DIM)
    vc = vc.reshape(B, NSA_GROUPS, n_chunks, HEAD_DIM)

    o_nsa = _nsa(rope_proj, plain_proj, kc, vc, small_proj, gate_bias, B, S)
    ccol, crow = _forget_cumsum(small_proj, fbias, B, S, ngate)
    o_fox = _fox(plain_proj, ccol, crow, B, S, ngate)

    merged = _merge(o_nsa.reshape(T, qw), o_fox.reshape(T, fw), w_up_nsa.astype(BF16), w_up_fox.astype(BF16), gates, D)
    return _matmul(merged, w_out.astype(BF16), F32, "residual", (x2,), name="outproj")


def _route_kernel(xn_ref, wq_ref, k1_ref, k2_ref, e_ref, g_ref, *, tm):
    h = pl.program_id(1)
    q = _dot(xn_ref[...], wq_ref[...])
    s1 = _dot_nt(q[:, :PEER_HALF].astype(BF16), k1_ref[...])
    s2 = _dot_nt(q[:, PEER_HALF:].astype(BF16), k2_ref[...])
    nc = PEER_TOPK * PEER_TOPK
    lane = lax.broadcasted_iota(I32, (tm, LANES), 1)
    lane_f = lane.astype(F32)
    clane = lax.broadcasted_iota(I32, (tm, nc), 1)
    clane_f = clane.astype(F32)

    def top_keys(s, cand_rank):
        cv = jnp.zeros((tm, nc), F32)
        ci = jnp.zeros((tm, nc), F32)
        for r in range(PEER_TOPK):
            m = jnp.max(s, axis=-1, keepdims=True)
            idx = jnp.min(jnp.where(s == m, lane_f, float(LANES)), axis=-1, keepdims=True)
            hit = cand_rank == r
            cv = jnp.where(hit, m, cv)
            ci = jnp.where(hit, idx, ci)
            s = jnp.where(lane_f == idx, -jnp.inf, s)
        return cv, ci

    cv1, ci1 = top_keys(s1, clane >> 4)
    cv2, ci2 = top_keys(s2, clane & (PEER_TOPK - 1))
    cand = cv1 + cv2
    cidx = ci1 * float(PEER_N_KEYS) + ci2

    top_s = jnp.zeros((tm, LANES), F32)
    top_e = jnp.zeros((tm, LANES), F32)
    for r in range(PEER_TOPK):
        m = jnp.max(cand, axis=-1, keepdims=True)
        pos = jnp.min(jnp.where(cand == m, clane_f, float(nc)), axis=-1, keepdims=True)
        hit = clane_f == pos
        e = jnp.sum(jnp.where(hit, cidx, 0.0), axis=-1, keepdims=True)
        here = lane == h * PEER_TOPK + r
        top_s = jnp.where(here, m, top_s)
        top_e = jnp.where(here, e, top_e)
        cand = jnp.where(hit, -jnp.inf, cand)

    mine = (lane >= h * PEER_TOPK) & (lane < (h + 1) * PEER_TOPK)
    m0 = jnp.max(jnp.where(mine, top_s, -jnp.inf), axis=-1, keepdims=True)
    ex = jnp.where(mine, jnp.exp(top_s - m0), 0.0)
    gsm = ex / jnp.sum(ex, axis=-1, keepdims=True)

    @pl.when(h == 0)
    def _():
        e_ref[...] = jnp.zeros_like(e_ref)
        g_ref[...] = jnp.zeros_like(g_ref)

    e_ref[...] = jnp.where(mine, top_e.astype(I32), e_ref[...])
    g_ref[...] = jnp.where(mine, gsm, g_ref[...])


def _route(xn_bf16, w_query, sub_keys_1, sub_keys_2, tm=128):
    T, D = xn_bf16.shape
    kd = 2 * PEER_HALF
    return pl.pallas_call(
        functools.partial(_route_kernel, tm=tm),
        grid=(T // tm, PEER_HEADS),
        in_specs=[
            pl.BlockSpec((tm, D), lambda i, h: (i, 0)),
            pl.BlockSpec((D, kd), lambda i, h: (0, h)),
            pl.BlockSpec((None, PEER_N_KEYS, PEER_HALF), lambda i, h: (h, 0, 0)),
            pl.BlockSpec((None, PEER_N_KEYS, PEER_HALF), lambda i, h: (h, 0, 0)),
        ],
        out_specs=[pl.BlockSpec((tm, PEER_SLOTS), lambda i, h: (i, 0))] * 2,
        out_shape=[jax.ShapeDtypeStruct((T, PEER_SLOTS), I32), jax.ShapeDtypeStruct((T, PEER_SLOTS), F32)],
        compiler_params=_cparams(("parallel", "arbitrary")),
        name="peer_route",
    )(xn_bf16, w_query.astype(BF16), sub_keys_1.astype(BF16), sub_keys_2.astype(BF16))


def _slot_sort_kernel(e_ref, g_ref, se_ref, sg_ref, meta_ref, *, tm, part_shift):
    rows = 8
    lane = lax.broadcasted_iota(I32, (rows, LANES), 1)

    def chunk(c, carry):
        r0 = pl.multiple_of(c * rows, rows)
        e = e_ref[pl.ds(r0, rows), :]
        g = g_ref[pl.ds(r0, rows), :]
        key = e * PEER_SLOTS + lane
        rank = jnp.zeros((rows, LANES), I32)
        for k in range(PEER_SLOTS):
            rank = rank + (key[:, k:k + 1] < key).astype(I32)
        se = jnp.zeros((rows, LANES), I32)
        sg = jnp.zeros((rows, LANES), F32)
        for k in range(PEER_SLOTS):
            hit = rank[:, k:k + 1] == lane
            se = jnp.where(hit, e[:, k:k + 1], se)
            sg = jnp.where(hit, g[:, k:k + 1], sg)
        part = se >> part_shift
        meta = jnp.zeros((rows, LANES), I32)
        start = jnp.zeros((rows, 1), I32)
        for p in range(PEER_PARTS):
            cnt = jnp.sum((part == p).astype(F32), axis=-1, keepdims=True).astype(I32)
            meta = jnp.where(lane == p, start, meta)
            meta = jnp.where(lane == PEER_PARTS + p, cnt, meta)
            start = start + cnt
        se_ref[pl.ds(r0, rows), :] = se
        sg_ref[pl.ds(r0, rows), :] = sg
        meta_ref[pl.ds(r0, rows), :] = meta
        return carry

    lax.fori_loop(0, tm // rows, chunk, 0)


def _slot_sort(e, g, part_shift, tm=256):
    T = e.shape[0]
    spec = pl.BlockSpec((tm, PEER_SLOTS), lambda i: (i, 0))
    return pl.pallas_call(
        functools.partial(_slot_sort_kernel, tm=tm, part_shift=part_shift),
        grid=(T // tm,),
        in_specs=[spec, spec],
        out_specs=[spec, spec, spec],
        out_shape=[jax.ShapeDtypeStruct((T, PEER_SLOTS), I32), jax.ShapeDtypeStruct((T, PEER_SLOTS), F32),
                   jax.ShapeDtypeStruct((T, PEER_SLOTS), I32)],
        compiler_params=_cparams(("parallel",)),
        name="peer_slot_sort",
    )(e, g)


PEER_UNROLL = 8
META_W = 2 * PEER_PARTS


def _peer_down_kernel(u_ref, x_ref, se_ref, meta_ref, sg_ref, w_ref, a_scr, *, tb, part_size):
    p = pl.program_id(0)
    lane8 = lax.broadcasted_iota(I32, (8, LANES), 1)
    lane1 = lax.broadcasted_iota(I32, (1, LANES), 1)

    def token(t, carry):
        start = meta_ref[t * META_W + p]
        cnt = meta_ref[t * META_W + PEER_PARTS + p]
        xt = x_ref[t]

        def group(gi, acc8):
            j0 = start + gi * PEER_UNROLL
            for i in range(PEER_UNROLL):
                j = j0 + i
                e = se_ref[t * PEER_SLOTS + jnp.minimum(j, PEER_SLOTS - 1)]
                u = u_ref[e & (part_size - 1)].astype(F32)
                pr = xt * u
                s = pr[0:8] + pr[8:16]
                acc8 = jnp.where(lane8 == j, jnp.sum(s, axis=-1, keepdims=True), acc8)
            return acc8

        acc8 = lax.fori_loop(0, lax.shift_right_logical(cnt + (PEER_UNROLL - 1), 3), group, jnp.zeros((8, LANES), F32))
        a_row = jnp.sum(acc8, axis=0, keepdims=True)
        valid = (lane1 >= start) & (lane1 < start + cnt)
        a_scr[pl.ds(t, 1), :] = jnp.where(valid, a_row, 0.0)
        return carry

    lax.fori_loop(0, tb, token, 0)
    w_ref[...] = jax.nn.gelu(a_scr[...]) * sg_ref[...]


def _peer_up_kernel(v_ref, y_ref, se_ref, meta_ref, w_ref, o_ref, *, tb, part_size):
    p = pl.program_id(0)

    def token(t, carry):
        start = meta_ref[t * META_W + p]
        cnt = meta_ref[t * META_W + PEER_PARTS + p]

        def group(gi, acc):
            j0 = start + gi * PEER_UNROLL
            for i in range(PEER_UNROLL):
                j = j0 + i
                jj = t * PEER_SLOTS + jnp.minimum(j, PEER_SLOTS - 1)
                w = jnp.where(j < start + cnt, w_ref[jj], 0.0)
                acc = acc + w * v_ref[se_ref[jj] & (part_size - 1)].astype(F32)
            return acc

        o_ref[t] = lax.fori_loop(0, lax.shift_right_logical(cnt + (PEER_UNROLL - 1), 3), group, y_ref[t])
        return carry

    lax.fori_loop(0, tb, token, 0)


def _peer_experts(xn3, resid3, se, sg, meta, down_tab, up_tab, tb=64):
    T = xn3.shape[0]
    E = down_tab.shape[0]
    part_size = E // PEER_PARTS
    nt = T // tb
    rows = xn3.shape[1]
    se_flat = se.reshape(T * PEER_SLOTS)
    meta_flat = meta[:, :META_W].reshape(T * META_W)
    tab_spec = pl.BlockSpec((part_size, rows, LANES), lambda p, i: (p, 0, 0))
    tok_spec = pl.BlockSpec((tb, rows, LANES), lambda p, i: (i, 0, 0))
    se_spec = pl.BlockSpec((tb * PEER_SLOTS,), lambda p, i: (i,), memory_space=pltpu.SMEM)
    meta_spec = pl.BlockSpec((tb * META_W,), lambda p, i: (i,), memory_space=pltpu.SMEM)
    w_parts = pl.pallas_call(
        functools.partial(_peer_down_kernel, tb=tb, part_size=part_size),
        grid=(PEER_PARTS, nt),
        in_specs=[tab_spec, tok_spec, se_spec, meta_spec, pl.BlockSpec((tb, PEER_SLOTS), lambda p, i: (i, 0))],
        out_specs=pl.BlockSpec((None, tb, PEER_SLOTS), lambda p, i: (p, i, 0)),
        out_shape=jax.ShapeDtypeStruct((PEER_PARTS, T, PEER_SLOTS), F32),
        scratch_shapes=[pltpu.VMEM((tb, PEER_SLOTS), F32)],
        compiler_params=_cparams(("arbitrary", "arbitrary")),
        name="peer_down",
    )(down_tab, xn3, se_flat, meta_flat, sg)
    w_flat = w_parts.reshape(PEER_PARTS * T * PEER_SLOTS)
    w_spec = pl.BlockSpec((tb * PEER_SLOTS,), lambda p, i: (p * nt + i,), memory_space=pltpu.SMEM)
    return pl.pallas_call(
        functools.partial(_peer_up_kernel, tb=tb, part_size=part_size),
        grid=(PEER_PARTS, nt),
        in_specs=[tab_spec, tok_spec, se_spec, meta_spec, w_spec],
        out_specs=tok_spec,
        out_shape=jax.ShapeDtypeStruct(resid3.shape, F32),
        input_output_aliases={1: 0},
        compiler_params=_cparams(("arbitrary", "arbitrary")),
        name="peer_up",
    )(up_tab, resid3, se_flat, meta_flat, w_flat)


def _peer(x2, norm_gain, w_query, sub_keys_1, sub_keys_2, expert_down, expert_up):
    T, D = x2.shape
    E = expert_down.shape[0]
    rows = D // LANES
    xn_bf16, xn = _rmsnorm(x2, norm_gain, (BF16, F32))
    e, g = _route(xn_bf16, w_query, sub_keys_1, sub_keys_2)
    part_shift = int(math.log2(E // PEER_PARTS))
    se, sg, meta = _slot_sort(e, g, part_shift)
    down_tab = expert_down.astype(BF16).reshape(E, rows, LANES)
    up_tab = expert_up.astype(BF16).reshape(E, rows, LANES)
    y = _peer_experts(xn.reshape(T, rows, LANES), x2.reshape(T, rows, LANES), se, sg, meta, down_tab, up_tab)
    return y.reshape(T, D)


def kernel(x, norm_mix_gain, w_in, nsa_gate_bias, fox_forget_bias, merge_gate_bias, k_cmp_pos, k_cmp_w1, k_cmp_w2, v_cmp_pos, v_cmp_w1, v_cmp_w2, w_up_nsa, w_up_fox, w_out, norm_ffn_gain, peer_w_query, peer_sub_keys_1, peer_sub_keys_2, peer_expert_down, peer_expert_up, norm_final_gain):
    B, S, D = x.shape
    x2 = x.reshape(B * S, D)
    for l in range(norm_mix_gain.shape[0]):
        x2 = _mixer(x2, B, S, norm_mix_gain[l], w_in[l], nsa_gate_bias[l], fox_forget_bias[l], merge_gate_bias[l],
                    k_cmp_pos[l], k_cmp_w1[l], k_cmp_w2[l], v_cmp_pos[l], v_cmp_w1[l], v_cmp_w2[l],
                    w_up_nsa[l], w_up_fox[l], w_out[l])
        x2 = _peer(x2, norm_ffn_gain[l], peer_w_query[l], peer_sub_keys_1[l], peer_sub_keys_2[l], peer_expert_down[l],
                   peer_expert_up[l])
    (out,) = _rmsnorm(x2, norm_final_gain, (F32,))
    return out.reshape(B, S, D)
```

```python
import functools
import math

import jax
import jax.numpy as jnp
import numpy as np
from jax import lax
from jax.experimental import pallas as pl
from jax.experimental.pallas import tpu as pltpu

F32 = jnp.float32
BF16 = jnp.bfloat16
I32 = jnp.int32

LANES = 128
HEAD_DIM = 128
ROPE_THETA = 10000.0
NORM_EPS = 1e-6
NEG_INF = -1e30
TINY = 1e-30
PAD_SCORE = -3e38

NSA_HEADS = 8
NSA_GROUPS = 2
NSA_HPG = NSA_HEADS // NSA_GROUPS
CMP_BLOCK = 32
CMP_STRIDE = 16
SEL_BLOCK = 64
SEL_TOPK = 16
FORCED_SCORE = 1e4
WINDOW = 512
FOX_HEADS = 8

PEER_HEADS = 8
PEER_N_KEYS = 128
PEER_HALF = 128
PEER_TOPK = 16
PEER_SLOTS = PEER_HEADS * PEER_TOPK
PEER_PARTS = 4

VMEM_LIMIT_BYTES = 56 * 1024 * 1024


def _cparams(sem):
    return pltpu.CompilerParams(dimension_semantics=sem, vmem_limit_bytes=VMEM_LIMIT_BYTES)


def _split3(x):
    hi = x.astype(BF16)
    r = x - hi.astype(F32)
    mid = r.astype(BF16)
    lo = (r - mid.astype(F32)).astype(BF16)
    return hi, mid, lo


def _dot(a, b):
    return jnp.dot(a, b, preferred_element_type=F32)


def _dot_nt(a, b):
    return lax.dot_general(a, b, (((1,), (1,)), ((), ())), preferred_element_type=F32)


def _dot3_right(x, m01):
    hi, mid, lo = _split3(x)
    return _dot(hi, m01) + _dot(mid, m01) + _dot(lo, m01)


def _dot3_left(m01, x):
    hi, mid, lo = _split3(x)
    return _dot(m01, hi) + _dot(m01, mid) + _dot(m01, lo)


def _masked_softmax(s, mask):
    s = jnp.where(mask, s, NEG_INF)
    m = jnp.max(s, axis=-1, keepdims=True)
    e = jnp.where(mask, jnp.exp(s - m), 0.0)
    return e / jnp.maximum(jnp.sum(e, axis=-1, keepdims=True), TINY)


def _rope_tile(x, cos_full, sin_signed):
    return x * cos_full + pltpu.roll(x, HEAD_DIM // 2, axis=1) * sin_signed


def _rmsnorm_kernel(x_ref, g_ref, *o_refs):
    x = x_ref[...]
    y = x * lax.rsqrt(jnp.mean(x * x, axis=-1, keepdims=True) + NORM_EPS) * g_ref[...]
    for o_ref in o_refs:
        o_ref[...] = y.astype(o_ref.dtype)


def _rmsnorm(x2, gain, out_dtypes, tm=512):
    T, D = x2.shape
    outs = pl.pallas_call(
        _rmsnorm_kernel,
        grid=(T // tm,),
        in_specs=[pl.BlockSpec((tm, D), lambda i: (i, 0)), pl.BlockSpec((1, D), lambda i: (0, 0))],
        out_specs=[pl.BlockSpec((tm, D), lambda i: (i, 0)) for _ in out_dtypes],
        out_shape=[jax.ShapeDtypeStruct((T, D), dt) for dt in out_dtypes],
        compiler_params=_cparams(("parallel",)),
        name="rmsnorm",
    )(x2, gain.reshape(1, D).astype(F32))
    return outs


def _mm_kernel(a_ref, b_ref, *rest, epilogue):
    o_ref = rest[-1]
    acc = _dot(a_ref[...], b_ref[...])
    if epilogue == "rope":
        cos_ref, sin_ref = rest[0], rest[1]
        c, s = cos_ref[...], sin_ref[...]
        for j in range(acc.shape[1] // HEAD_DIM):
            sl = slice(j * HEAD_DIM, (j + 1) * HEAD_DIM)
            o_ref[:, sl] = _rope_tile(acc[:, sl], c, s).astype(o_ref.dtype)
    elif epilogue == "sigmoid_bias":
        o_ref[...] = jax.nn.sigmoid(acc + rest[0][...]).astype(o_ref.dtype)
    elif epilogue == "residual":
        o_ref[...] = (rest[0][...] + acc).astype(o_ref.dtype)
    else:
        o_ref[...] = acc.astype(o_ref.dtype)


def _matmul(a, b, out_dtype, epilogue="none", extras=(), tm=1024, tn=512, seq=None, name="matmul"):
    M, K = a.shape
    _, N = b.shape
    tn = min(tn, N)
    tm = min(tm, M)
    in_specs = [pl.BlockSpec((tm, K), lambda i, j: (i, 0)), pl.BlockSpec((K, tn), lambda i, j: (0, j))]
    if epilogue == "rope":
        nrep = seq // tm
        in_specs += [pl.BlockSpec((tm, HEAD_DIM), lambda i, j: (i % nrep, 0))] * 2
    elif epilogue == "sigmoid_bias":
        in_specs += [pl.BlockSpec((1, tn), lambda i, j: (0, j))]
    elif epilogue == "residual":
        in_specs += [pl.BlockSpec((tm, tn), lambda i, j: (i, j))]
    return pl.pallas_call(
        functools.partial(_mm_kernel, epilogue=epilogue),
        grid=(M // tm, N // tn),
        in_specs=in_specs,
        out_specs=pl.BlockSpec((tm, tn), lambda i, j: (i, j)),
        out_shape=jax.ShapeDtypeStruct((M, N), out_dtype),
        compiler_params=_cparams(("parallel", "parallel")),
        name=name,
    )(a, b, *extras)


def _compress_kernel(ch_ref, pos_ref, w1t_ref, w1b_ref, w2_ref, cos_ref, sin_ref, o_ref, *, rope):
    ch = ch_ref[...].astype(F32)
    a_top = (ch + pos_ref[0:1, :]).astype(BF16)
    a_bot = (ch + pos_ref[1:2, :]).astype(BF16)
    y_top = _dot(a_top, w1t_ref[...])
    y_bot = _dot(a_bot, w1b_ref[...])
    n = y_bot.shape[0]
    hidden = y_top + pltpu.roll(y_bot, n - 1, axis=0)
    out = _dot(jax.nn.gelu(hidden).astype(BF16), w2_ref[...])
    if rope:
        out = _rope_tile(out, cos_ref[...], sin_ref[...])
    row = lax.broadcasted_iota(I32, out.shape, 0)
    o_ref[...] = jnp.where(row < n - 1, out, 0.0).astype(o_ref.dtype)


def _compress(chunks, pos, w1, w2, cos_c, sin_c, rope):
    BG, NC, CK = chunks.shape
    hid = w1.shape[1]
    pos2 = pos.reshape(2, CK).astype(F32)
    return pl.pallas_call(
        functools.partial(_compress_kernel, rope=rope),
        grid=(BG,),
        in_specs=[
            pl.BlockSpec((None, NC, CK), lambda i: (i, 0, 0)),
            pl.BlockSpec((2, CK), lambda i: (0, 0)),
            pl.BlockSpec((CK, hid), lambda i: (0, 0)),
            pl.BlockSpec((CK, hid), lambda i: (1, 0)),
            pl.BlockSpec((hid, HEAD_DIM), lambda i: (0, 0)),
            pl.BlockSpec((NC, HEAD_DIM), lambda i: (0, 0)),
            pl.BlockSpec((NC, HEAD_DIM), lambda i: (0, 0)),
        ],
        out_specs=pl.BlockSpec((None, NC, HEAD_DIM), lambda i: (i, 0, 0)),
        out_shape=jax.ShapeDtypeStruct((BG, NC, HEAD_DIM), BF16),
        compiler_params=_cparams(("parallel",)),
        name="compress",
    )(chunks, pos2, w1.astype(BF16), w1.astype(BF16), w2.astype(BF16), cos_c, sin_c)


def _nsa_kernel(q_ref, ks_ref, kw_ref, vs_ref, vw_ref, kc_ref, vc_ref, gate_ref, gbias_ref, ovl_ref, exp_ref,
                o_ref, *, tq, seq):
    i = pl.program_id(1)
    t0 = i * tq
    scale = HEAD_DIM ** -0.5
    n_sel = seq // SEL_BLOCK
    n_cmp = seq // CMP_STRIDE - CMP_BLOCK // CMP_STRIDE + 1
    wlen = tq + WINDOW

    t_col = t0 + lax.broadcasted_iota(I32, (tq, 1), 0)
    lane = lax.broadcasted_iota(I32, (tq, LANES), 1)
    gate = jax.nn.sigmoid(gate_ref[...] + gbias_ref[...])

    cmp_mask = (lane * CMP_STRIDE + (CMP_BLOCK - 1) <= t_col) & (lane < n_cmp)
    key_all = lax.broadcasted_iota(I32, (tq, seq), 1)
    causal = key_all <= t_col
    wstart = pl.multiple_of(jnp.maximum(t0 - WINDOW, 0), tq)
    key_w = wstart + lax.broadcasted_iota(I32, (tq, wlen), 1)
    win_mask = (key_w <= t_col) & (t_col - key_w < WINDOW)

    cur = t_col >> 6
    forced = (lane == 0) | (lane == cur) | (lane == cur - 1)
    future = lane * SEL_BLOCK > t_col

    for g in range(NSA_GROUPS):
        gs = slice(g * HEAD_DIM, (g + 1) * HEAD_DIM)
        kc = kc_ref[g]
        vc = vc_ref[g]
        o_cmp = []
        psum = jnp.zeros((tq, LANES), F32)
        for h in range(NSA_HPG):
            hs = slice((g * NSA_HPG + h) * HEAD_DIM, (g * NSA_HPG + h + 1) * HEAD_DIM)
            p = _masked_softmax(_dot_nt(q_ref[:, hs], kc) * scale, cmp_mask)
            psum = psum + p
            o_cmp.append(_dot(p.astype(BF16), vc))
        imp = _dot3_right(psum, ovl_ref[...])
        score = jnp.where(forced, FORCED_SCORE, imp)
        score = jnp.where(future, NEG_INF, score)
        score = jnp.where(lane < n_sel, score, PAD_SCORE)
        rank = jnp.zeros((tq, LANES), I32)
        for k in range(n_sel):
            ck = score[:, k:k + 1]
            beats = (ck > score) | ((ck == score) & (lane > k))
            rank = rank + beats.astype(I32)
        sel = ((rank < min(SEL_TOPK, n_sel)) & (lane < n_sel)).astype(BF16)
        sel_keys = _dot(sel, exp_ref[...]) > 0.5
        sel_mask = sel_keys & causal

        ks = ks_ref[:, gs]
        vs = vs_ref[:, gs]
        kw = kw_ref[pl.ds(wstart, wlen), gs]
        vw = vw_ref[pl.ds(wstart, wlen), gs]
        for h in range(NSA_HPG):
            hh = g * NSA_HPG + h
            hs = slice(hh * HEAD_DIM, (hh + 1) * HEAD_DIM)
            q = q_ref[:, hs]
            p = _masked_softmax(_dot_nt(q, ks) * scale, sel_mask)
            o_sel = _dot(p.astype(BF16), vs)
            p = _masked_softmax(_dot_nt(q, kw) * scale, win_mask)
            o_win = _dot(p.astype(BF16), vw)
            gl = hh * 3
            o = (gate[:, gl:gl + 1] * o_cmp[h] + gate[:, gl + 1:gl + 2] * o_sel + gate[:, gl + 2:gl + 3] * o_win)
            o_ref[:, hs] = o.astype(o_ref.dtype)


def _nsa(rope_proj, plain_proj, kc, vc, small_proj, gate_bias, B, S, tq=256):
    n_sel = S // SEL_BLOCK
    n_cmp = S // CMP_STRIDE - CMP_BLOCK // CMP_STRIDE + 1
    c_start = np.arange(LANES) * CMP_STRIDE
    s_start = np.arange(LANES) * SEL_BLOCK
    ovl = (c_start[:, None] < s_start[None, :] + SEL_BLOCK) & (s_start[None, :] < c_start[:, None] + CMP_BLOCK)
    ovl &= (np.arange(LANES)[:, None] < n_cmp) & (np.arange(LANES)[None, :] < n_sel)
    expand = (np.arange(LANES)[:, None] == (np.arange(S)[None, :] // SEL_BLOCK))
    rp = rope_proj.reshape(B, S, -1)
    pp = plain_proj.reshape(B, S, -1)
    sm = small_proj.reshape(B, S, LANES)
    gw = NSA_GROUPS * HEAD_DIM
    qw = NSA_HEADS * HEAD_DIM
    return pl.pallas_call(
        functools.partial(_nsa_kernel, tq=tq, seq=S),
        grid=(B, S // tq),
        in_specs=[
            pl.BlockSpec((None, tq, qw), lambda b, i: (b, i, 0)),
            pl.BlockSpec((None, S, gw), lambda b, i: (b, 0, qw // gw)),
            pl.BlockSpec((None, S, gw), lambda b, i: (b, 0, qw // gw + 1)),
            pl.BlockSpec((None, S, gw), lambda b, i: (b, 0, 2)),
            pl.BlockSpec((None, S, gw), lambda b, i: (b, 0, 3)),
            pl.BlockSpec((None, NSA_GROUPS, LANES, HEAD_DIM), lambda b, i: (b, 0, 0, 0)),
            pl.BlockSpec((None, NSA_GROUPS, LANES, HEAD_DIM), lambda b, i: (b, 0, 0, 0)),
            pl.BlockSpec((None, tq, LANES), lambda b, i: (b, i, 0)),
            pl.BlockSpec((1, LANES), lambda b, i: (0, 0)),
            pl.BlockSpec((LANES, LANES), lambda b, i: (0, 0)),
            pl.BlockSpec((LANES, S), lambda b, i: (0, 0)),
        ],
        out_specs=pl.BlockSpec((None, tq, qw), lambda b, i: (b, i, 0)),
        out_shape=jax.ShapeDtypeStruct((B, S, qw), BF16),
        compiler_params=_cparams(("parallel", "parallel")),
        name="nsa",
    )(rp, rp, rp, pp, pp, kc, vc, sm, gate_bias, jnp.asarray(ovl, BF16), jnp.asarray(expand, BF16))


def _forget_cumsum_kernel(sm_ref, fbias_ref, tri_ref, ccol_ref, crow_ref, *, seq, lane0):
    x = sm_ref[...] + fbias_ref[...]
    lane = lax.broadcasted_iota(I32, x.shape, 1)
    log_f = jnp.minimum(x, 0.0) - jnp.log1p(jnp.exp(-jnp.abs(x)))
    log_f = jnp.where((lane >= lane0) & (lane < lane0 + FOX_HEADS), log_f, 0.0)
    carry = jnp.zeros((1, LANES), F32)
    blk = tri_ref.shape[0]
    for r in range(seq // blk):
        c = _dot3_left(tri_ref[...], log_f[r * blk:(r + 1) * blk, :]) + carry
        ccol_ref[r * blk:(r + 1) * blk, :] = c
        carry = c[blk - 1:blk, :]
    crow_ref[...] = ccol_ref[...].T[lane0:lane0 + FOX_HEADS, :]


def _forget_cumsum(small_proj, fbias, B, S, lane0):
    tri = np.tril(np.ones((LANES, LANES), np.float32))
    return pl.pallas_call(
        functools.partial(_forget_cumsum_kernel, seq=S, lane0=lane0),
        grid=(B,),
        in_specs=[
            pl.BlockSpec((None, S, LANES), lambda b: (b, 0, 0)),
            pl.BlockSpec((1, LANES), lambda b: (0, 0)),
            pl.BlockSpec((LANES, LANES), lambda b: (0, 0)),
        ],
        out_specs=[
            pl.BlockSpec((None, S, LANES), lambda b: (b, 0, 0)),
            pl.BlockSpec((None, FOX_HEADS, S), lambda b: (b, 0, 0)),
        ],
        out_shape=[jax.ShapeDtypeStruct((B, S, LANES), F32), jax.ShapeDtypeStruct((B, FOX_HEADS, S), F32)],
        compiler_params=_cparams(("parallel",)),
        name="forget_cumsum",
    )(small_proj.reshape(B, S, LANES), fbias, jnp.asarray(tri, BF16))


def _fox_kernel(q_ref, k_ref, v_ref, ccol_ref, crow_ref, o_ref, *, tq, seq, lane0):
    i = pl.program_id(1)
    t_col = i * tq + lax.broadcasted_iota(I32, (tq, 1), 0)
    causal = lax.broadcasted_iota(I32, (tq, seq), 1) <= t_col
    scale = HEAD_DIM ** -0.5
    for h in range(FOX_HEADS):
        hs = slice(h * HEAD_DIM, (h + 1) * HEAD_DIM)
        s = _dot_nt(q_ref[:, hs], k_ref[:, hs]) * scale
        s = s + ccol_ref[:, lane0 + h:lane0 + h + 1] - crow_ref[h:h + 1, :]
        p = _masked_softmax(s, causal)
        o_ref[:, hs] = _dot(p.astype(BF16), v_ref[:, hs]).astype(o_ref.dtype)


def _fox(plain_proj, ccol, crow, B, S, lane0, tq=256):
    pp = plain_proj.reshape(B, S, -1)
    fw = FOX_HEADS * HEAD_DIM
    return pl.pallas_call(
        functools.partial(_fox_kernel, tq=tq, seq=S, lane0=lane0),
        grid=(B, S // tq),
        in_specs=[
            pl.BlockSpec((None, tq, fw), lambda b, i: (b, i, 1)),
            pl.BlockSpec((None, S, fw), lambda b, i: (b, 0, 2)),
            pl.BlockSpec((None, S, fw), lambda b, i: (b, 0, 3)),
            pl.BlockSpec((None, tq, LANES), lambda b, i: (b, i, 0)),
            pl.BlockSpec((None, FOX_HEADS, S), lambda b, i: (b, 0, 0)),
        ],
        out_specs=pl.BlockSpec((None, tq, fw), lambda b, i: (b, i, 0)),
        out_shape=jax.ShapeDtypeStruct((B, S, fw), BF16),
        compiler_params=_cparams(("parallel", "parallel")),
        name="fox",
    )(pp, pp, pp, ccol, crow)


def _merge_kernel(on_ref, of_ref, wn_ref, wf_ref, ga_ref, gb_ref, o_ref):
    a = _dot(on_ref[...], wn_ref[...])
    b = _dot(of_ref[...], wf_ref[...])
    o_ref[...] = (ga_ref[...].astype(F32) * a + gb_ref[...].astype(F32) * b).astype(o_ref.dtype)


def _merge(o_nsa, o_fox, w_up_nsa, w_up_fox, gates, d_model, tm=1024, tn=512):
    T, K = o_nsa.shape
    nb = d_model // tn
    return pl.pallas_call(
        _merge_kernel,
        grid=(T // tm, nb),
        in_specs=[
            pl.BlockSpec((tm, K), lambda i, j: (i, 0)),
            pl.BlockSpec((tm, K), lambda i, j: (i, 0)),
            pl.BlockSpec((K, tn), lambda i, j: (0, j)),
            pl.BlockSpec((K, tn), lambda i, j: (0, j)),
            pl.BlockSpec((tm, tn), lambda i, j: (i, j)),
            pl.BlockSpec((tm, tn), lambda i, j: (i, j + nb)),
        ],
        out_specs=pl.BlockSpec((tm, tn), lambda i, j: (i, j)),
        out_shape=jax.ShapeDtypeStruct((T, d_model), BF16),
        compiler_params=_cparams(("parallel", "parallel")),
        name="merge",
    )(o_nsa, o_fox, w_up_nsa, w_up_fox, gates, gates)


def _rope_tables(pos):
    inv_freq = ROPE_THETA ** (-jnp.arange(0, HEAD_DIM, 2, dtype=F32) / HEAD_DIM)
    ang = pos.astype(F32)[:, None] * inv_freq[None, :]
    c, s = jnp.cos(ang), jnp.sin(ang)
    return jnp.concatenate([c, c], axis=-1), jnp.concatenate([-s, s], axis=-1)


def _mixer(x2, B, S, norm_gain, w_in, nsa_gate_bias, fox_forget_bias, merge_gate_bias, k_cmp_pos, k_cmp_w1, k_cmp_w2,
           v_cmp_pos, v_cmp_w1, v_cmp_w2, w_up_nsa, w_up_fox, w_out):
    T, D = x2.shape
    qw = NSA_HEADS * HEAD_DIM
    gw = NSA_GROUPS * HEAD_DIM
    fw = FOX_HEADS * HEAD_DIM
    ngate = 3 * NSA_HEADS
    widths = (qw, gw, gw, gw, gw, gw, gw, ngate, fw, fw, fw, FOX_HEADS, D, D)
    offs = np.concatenate([[0], np.cumsum(widths)])
    (w_qn, w_kc, w_vc, w_ks, w_vs, w_kw, w_vw, w_gn, w_qf, w_kf, w_vf, w_ff, w_ga, w_gb) = [
        w_in[:, offs[j]:offs[j + 1]] for j in range(len(widths))]
    w_rope = jnp.concatenate([w_qn, w_ks, w_kw], axis=1).astype(BF16)
    w_plain = jnp.concatenate([w_kc, w_vc, w_vs, w_vw, w_qf, w_kf, w_vf], axis=1).astype(BF16)
    w_gate = jnp.concatenate([w_ga, w_gb], axis=1).astype(BF16)
    pad = LANES - ngate - FOX_HEADS
    w_small = jnp.concatenate([w_gn, w_ff, jnp.zeros((D, pad), w_in.dtype)], axis=1).astype(BF16)
    gate_bias = jnp.concatenate([nsa_gate_bias, jnp.zeros((LANES - ngate,), F32)]).reshape(1, LANES)
    fbias = jnp.concatenate([jnp.zeros((ngate,), F32), fox_forget_bias, jnp.zeros((pad,), F32)]).reshape(1, LANES)

    (h,) = _rmsnorm(x2, norm_gain, (BF16,))
    cos_t, sin_t = _rope_tables(jnp.arange(S))
    rope_proj = _matmul(h, w_rope, BF16, "rope", (cos_t, sin_t), seq=S, name="inproj_rope")
    plain_proj = _matmul(h, w_plain, BF16, name="inproj_plain")
    gates = _matmul(h, w_gate, BF16, "sigmoid_bias", (merge_gate_bias.reshape(1, 2 * D),), name="inproj_gate")
    small_proj = _matmul(h, w_small, F32, name="inproj_small")

    n_chunks = S // CMP_STRIDE
    ck = CMP_STRIDE * HEAD_DIM

    def chunks(col0):
        c = plain_proj[:, col0:col0 + gw].reshape(B, S, NSA_GROUPS, HEAD_DIM).transpose(0, 2, 1, 3)
        return c.reshape(B * NSA_GROUPS, n_chunks, ck)

    cos_c, sin_c = _rope_tables(jnp.arange(n_chunks) * CMP_STRIDE + CMP_BLOCK - 1)
    kc = _compress(chunks(0), k_cmp_pos, k_cmp_w1, k_cmp_w2, cos_c, sin_c, True)
    vc = _compress(chunks(gw), v_cmp_pos, v_cmp_w1, v_cmp_w2, cos_c, sin_c, False)
    kc = kc.reshape(B, NSA_GROUPS, n_chunks, HEAD_DIM)
    vc = vc.reshape(B, NSA_GROUPS, n_chunks, HEAD_DIM)

    o_nsa = _nsa(rope_proj, plain_proj, kc, vc, small_proj, gate_bias, B, S)
    ccol, crow = _forget_cumsum(small_proj, fbias, B, S, ngate)
    o_fox = _fox(plain_proj, ccol, crow, B, S, ngate)

    merged = _merge(o_nsa.reshape(T, qw), o_fox.reshape(T, fw), w_up_nsa.astype(BF16), w_up_fox.astype(BF16), gates, D)
    return _matmul(merged, w_out.astype(BF16), F32, "residual", (x2,), name="outproj")


def _route_kernel(xn_ref, wq_ref, k1_ref, k2_ref, e_ref, g_ref, *, tm):
    h = pl.program_id(1)
    q = _dot(xn_ref[...], wq_ref[...])
    s1 = _dot_nt(q[:, :PEER_HALF].astype(BF16), k1_ref[...])
    s2 = _dot_nt(q[:, PEER_HALF:].astype(BF16), k2_ref[...])
    nc = PEER_TOPK * PEER_TOPK
    lane = lax.broadcasted_iota(I32, (tm, LANES), 1)
    lane_f = lane.astype(F32)
    clane = lax.broadcasted_iota(I32, (tm, nc), 1)
    clane_f = clane.astype(F32)

    def top_keys(s, cand_rank):
        cv = jnp.zeros((tm, nc), F32)
        ci = jnp.zeros((tm, nc), F32)
        for r in range(PEER_TOPK):
            m = jnp.max(s, axis=-1, keepdims=True)
            idx = jnp.min(jnp.where(s == m, lane_f, float(LANES)), axis=-1, keepdims=True)
            hit = cand_rank == r
            cv = jnp.where(hit, m, cv)
            ci = jnp.where(hit, idx, ci)
            s = jnp.where(lane_f == idx, -jnp.inf, s)
        return cv, ci

    cv1, ci1 = top_keys(s1, clane >> 4)
    cv2, ci2 = top_keys(s2, clane & (PEER_TOPK - 1))
    cand = cv1 + cv2
    cidx = ci1 * float(PEER_N_KEYS) + ci2

    top_s = jnp.zeros((tm, LANES), F32)
    top_e = jnp.zeros((tm, LANES), F32)
    for r in range(PEER_TOPK):
        m = jnp.max(cand, axis=-1, keepdims=True)
        pos = jnp.min(jnp.where(cand == m, clane_f, float(nc)), axis=-1, keepdims=True)
        hit = clane_f == pos
        e = jnp.sum(jnp.where(hit, cidx, 0.0), axis=-1, keepdims=True)
        here = lane == h * PEER_TOPK + r
        top_s = jnp.where(here, m, top_s)
        top_e = jnp.where(here, e, top_e)
        cand = jnp.where(hit, -jnp.inf, cand)

    mine = (lane >= h * PEER_TOPK) & (lane < (h + 1) * PEER_TOPK)
    m0 = jnp.max(jnp.where(mine, top_s, -jnp.inf), axis=-1, keepdims=True)
    ex = jnp.where(mine, jnp.exp(top_s - m0), 0.0)
    gsm = ex / jnp.sum(ex, axis=-1, keepdims=True)

    @pl.when(h == 0)
    def _():
        e_ref[...] = jnp.zeros_like(e_ref)
        g_ref[...] = jnp.zeros_like(g_ref)

    e_ref[...] = jnp.where(mine, top_e.astype(I32), e_ref[...])
    g_ref[...] = jnp.where(mine, gsm, g_ref[...])


def _route(xn_bf16, w_query, sub_keys_1, sub_keys_2, tm=128):
    T, D = xn_bf16.shape
    kd = 2 * PEER_HALF
    return pl.pallas_call(
        functools.partial(_route_kernel, tm=tm),
        grid=(T // tm, PEER_HEADS),
        in_specs=[
            pl.BlockSpec((tm, D), lambda i, h: (i, 0)),
            pl.BlockSpec((D, kd), lambda i, h: (0, h)),
            pl.BlockSpec((None, PEER_N_KEYS, PEER_HALF), lambda i, h: (h, 0, 0)),
            pl.BlockSpec((None, PEER_N_KEYS, PEER_HALF), lambda i, h: (h, 0, 0)),
        ],
        out_specs=[pl.BlockSpec((tm, PEER_SLOTS), lambda i, h: (i, 0))] * 2,
        out_shape=[jax.ShapeDtypeStruct((T, PEER_SLOTS), I32), jax.ShapeDtypeStruct((T, PEER_SLOTS), F32)],
        compiler_params=_cparams(("parallel", "arbitrary")),
        name="peer_route",
    )(xn_bf16, w_query.astype(BF16), sub_keys_1.astype(BF16), sub_keys_2.astype(BF16))


def _slot_sort_kernel(e_ref, g_ref, se_ref, sg_ref, meta_ref, *, tm, part_shift):
    rows = 8
    lane = lax.broadcasted_iota(I32, (rows, LANES), 1)

    def chunk(c, carry):
        r0 = pl.multiple_of(c * rows, rows)
        e = e_ref[pl.ds(r0, rows), :]
        g = g_ref[pl.ds(r0, rows), :]
        key = e * PEER_SLOTS + lane
        rank = jnp.zeros((rows, LANES), I32)
        for k in range(PEER_SLOTS):
            rank = rank + (key[:, k:k + 1] < key).astype(I32)
        se = jnp.zeros((rows, LANES), I32)
        sg = jnp.zeros((rows, LANES), F32)
        for k in range(PEER_SLOTS):
            hit = rank[:, k:k + 1] == lane
            se = jnp.where(hit, e[:, k:k + 1], se)
            sg = jnp.where(hit, g[:, k:k + 1], sg)
        part = se >> part_shift
        meta = jnp.zeros((rows, LANES), I32)
        start = jnp.zeros((rows, 1), I32)
        for p in range(PEER_PARTS):
            cnt = jnp.sum((part == p).astype(F32), axis=-1, keepdims=True).astype(I32)
            meta = jnp.where(lane == p, start, meta)
            meta = jnp.where(lane == PEER_PARTS + p, cnt, meta)
            start = start + cnt
        se_ref[pl.ds(r0, rows), :] = se & ((1 << part_shift) - 1)
        sg_ref[pl.ds(r0, rows), :] = sg
        meta_ref[pl.ds(r0, rows), :] = meta
        return carry

    lax.fori_loop(0, tm // rows, chunk, 0)


def _slot_sort(e, g, part_shift, tm=256):
    T = e.shape[0]
    spec = pl.BlockSpec((tm, PEER_SLOTS), lambda i: (i, 0))
    return pl.pallas_call(
        functools.partial(_slot_sort_kernel, tm=tm, part_shift=part_shift),
        grid=(T // tm,),
        in_specs=[spec, spec],
        out_specs=[spec, spec, spec],
        out_shape=[jax.ShapeDtypeStruct((T, PEER_SLOTS), I32), jax.ShapeDtypeStruct((T, PEER_SLOTS), F32),
                   jax.ShapeDtypeStruct((T, PEER_SLOTS), I32)],
        compiler_params=_cparams(("parallel",)),
        name="peer_slot_sort",
    )(e, g)


SUBLANES = 8
META_W = 2 * PEER_PARTS
_BITREV3 = (0, 4, 2, 6, 1, 5, 3, 7)


def _slot_groups(meta_ref, t, p):
    start = meta_ref[t * META_W + p]
    cnt = meta_ref[t * META_W + PEER_PARTS + p]
    g0 = lax.shift_right_logical(start, 3)
    g1 = jnp.where(cnt > 0, lax.shift_right_logical(start + cnt + (SUBLANES - 1), 3), g0)
    return start, cnt, g0, g1


def _fold_rows(a, b, first, sh):
    return jnp.where(first, a + pltpu.roll(a, SUBLANES - sh, axis=0), b + pltpu.roll(b, sh, axis=0))


def _peer_down_kernel(u_ref, x_ref, se_ref, meta_ref, sg_ref, w_ref, r_scr, a_scr, *, tb):
    p = pl.program_id(0)
    sub = lax.broadcasted_iota(I32, (SUBLANES, LANES), 0)
    m4, m2, m1 = sub < 4, (sub & 2) == 0, (sub & 1) == 0
    lane1 = lax.broadcasted_iota(I32, (1, LANES), 1)

    @pl.when((p == 0) & (pl.program_id(1) == 0))
    def _():
        r_scr[...] = jnp.zeros_like(r_scr)

    def token(t, carry):
        _, _, g0, g1 = _slot_groups(meta_ref, t, p)
        xt = x_ref[t]

        def group(g, c2):
            base = pl.multiple_of(t * PEER_SLOTS + g * SUBLANES, SUBLANES)
            parts = []
            for k in range(SUBLANES):
                pr = xt * u_ref[se_ref[base + _BITREV3[k]]].astype(F32)
                parts.append(pr[0:8] + pr[8:16])
            q = [_fold_rows(parts[2 * m], parts[2 * m + 1], m4, 4) for m in range(4)]
            h = [_fold_rows(q[0], q[1], m2, 2), _fold_rows(q[2], q[3], m2, 2)]
            r_scr[pl.ds(base, SUBLANES), :] = _fold_rows(h[0], h[1], m1, 1)
            return c2

        lax.fori_loop(g0, g1, group, 0)
        return carry

    lax.fori_loop(0, tb, token, 0)

    def finish(b, carry):
        for tl in range(SUBLANES):
            t = b * SUBLANES + tl
            start = meta_ref[t * META_W + p]
            cnt = meta_ref[t * META_W + PEER_PARTS + p]
            r_t = r_scr[pl.ds(pl.multiple_of(t * PEER_SLOTS, PEER_SLOTS), PEER_SLOTS), :]
            a_row = jnp.sum(r_t.T, axis=0, keepdims=True)
            valid = (lane1 >= start) & (lane1 < start + cnt)
            a_scr[pl.ds(t, 1), :] = jnp.where(valid, a_row, 0.0)
        return carry

    lax.fori_loop(0, tb // SUBLANES, finish, 0)
    w_ref[...] = jax.nn.gelu(a_scr[...]) * sg_ref[...]


def _peer_up_kernel(v_ref, y_ref, se_ref, meta_ref, w_ref, o_ref, *, tb):
    p = pl.program_id(0)

    def token(t, carry):
        _, _, g0, g1 = _slot_groups(meta_ref, t, p)

        def group(g, acc):
            base = t * PEER_SLOTS + g * SUBLANES
            for i in range(SUBLANES):
                acc = acc + w_ref[base + i] * v_ref[se_ref[base + i]].astype(F32)
            return acc

        o_ref[t] = lax.fori_loop(g0, g1, group, y_ref[t])
        return carry

    lax.fori_loop(0, tb, token, 0)


def _peer_experts(xn3, resid3, se, sg, meta, down_tab, up_tab, tb=64):
    T = xn3.shape[0]
    E = down_tab.shape[0]
    part_size = E // PEER_PARTS
    nt = T // tb
    rows = xn3.shape[1]
    se_flat = se.reshape(T * PEER_SLOTS)
    meta_flat = meta[:, :META_W].reshape(T * META_W)
    tab_spec = pl.BlockSpec((part_size, rows, LANES), lambda p, i: (p, 0, 0))
    tok_spec = pl.BlockSpec((tb, rows, LANES), lambda p, i: (i, 0, 0))
    se_spec = pl.BlockSpec((tb * PEER_SLOTS,), lambda p, i: (i,), memory_space=pltpu.SMEM)
    meta_spec = pl.BlockSpec((tb * META_W,), lambda p, i: (i,), memory_space=pltpu.SMEM)
    w_parts = pl.pallas_call(
        functools.partial(_peer_down_kernel, tb=tb),
        grid=(PEER_PARTS, nt),
        in_specs=[tab_spec, tok_spec, se_spec, meta_spec, pl.BlockSpec((tb, PEER_SLOTS), lambda p, i: (i, 0))],
        out_specs=pl.BlockSpec((None, tb, PEER_SLOTS), lambda p, i: (p, i, 0)),
        out_shape=jax.ShapeDtypeStruct((PEER_PARTS, T, PEER_SLOTS), F32),
        scratch_shapes=[pltpu.VMEM((tb * PEER_SLOTS, LANES), F32), pltpu.VMEM((tb, PEER_SLOTS), F32)],
        compiler_params=_cparams(("arbitrary", "arbitrary")),
        name="peer_down",
    )(down_tab, xn3, se_flat, meta_flat, sg)
    w_flat = w_parts.reshape(PEER_PARTS * T * PEER_SLOTS)
    w_spec = pl.BlockSpec((tb * PEER_SLOTS,), lambda p, i: (p * nt + i,), memory_space=pltpu.SMEM)
    return pl.pallas_call(
        functools.partial(_peer_up_kernel, tb=tb),
        grid=(PEER_PARTS, nt),
        in_specs=[tab_spec, tok_spec, se_spec, meta_spec, w_spec],
        out_specs=tok_spec,
        out_shape=jax.ShapeDtypeStruct(resid3.shape, F32),
        input_output_aliases={1: 0},
        compiler_params=_cparams(("arbitrary", "arbitrary")),
        name="peer_up",
    )(up_tab, resid3, se_flat, meta_flat, w_flat)


def _peer(x2, norm_gain, w_query, sub_keys_1, sub_keys_2, expert_down, expert_up):
    T, D = x2.shape
    E = expert_down.shape[0]
    rows = D // LANES
    xn_bf16, xn = _rmsnorm(x2, norm_gain, (BF16, F32))
    e, g = _route(xn_bf16, w_query, sub_keys_1, sub_keys_2)
    part_shift = int(math.log2(E // PEER_PARTS))
    se, sg, meta = _slot_sort(e, g, part_shift)
    down_tab = expert_down.astype(BF16).reshape(E, rows, LANES)
    up_tab = expert_up.astype(BF16).reshape(E, rows, LANES)
    y = _peer_experts(xn.reshape(T, rows, LANES), x2.reshape(T, rows, LANES), se, sg, meta, down_tab, up_tab)
    return y.reshape(T, D)


def kernel(x, norm_mix_gain, w_in, nsa_gate_bias, fox_forget_bias, merge_gate_bias, k_cmp_pos, k_cmp_w1, k_cmp_w2, v_cmp_pos, v_cmp_w1, v_cmp_w2, w_up_nsa, w_up_fox, w_out, norm_ffn_gain, peer_w_query, peer_sub_keys_1, peer_sub_keys_2, peer_expert_down, peer_expert_up, norm_final_gain):
    B, S, D = x.shape
    x2 = x.reshape(B * S, D)
    for l in range(norm_mix_gain.shape[0]):
        x2 = _mixer(x2, B, S, norm_mix_gain[l], w_in[l], nsa_gate_bias[l], fox_forget_bias[l], merge_gate_bias[l],
                    k_cmp_pos[l], k_cmp_w1[l], k_cmp_w2[l], v_cmp_pos[l], v_cmp_w1[l], v_cmp_w2[l],
                    w_up_nsa[l], w_up_fox[l], w_out[l])
        x2 = _peer(x2, norm_ffn_gain[l], peer_w_query[l], peer_sub_keys_1[l], peer_sub_keys_2[l], peer_expert_down[l],
                   peer_expert_up[l])
    (out,) = _rmsnorm(x2, norm_final_gain, (F32,))
    return out.reshape(B, S, D)
```

```python
import functools
import math

import jax
import jax.numpy as jnp
import numpy as np
from jax import lax
from jax.experimental import pallas as pl
from jax.experimental.pallas import tpu as pltpu

F32 = jnp.float32
BF16 = jnp.bfloat16
I32 = jnp.int32

LANES = 128
HEAD_DIM = 128
ROPE_THETA = 10000.0
NORM_EPS = 1e-6
NEG_INF = -1e30
TINY = 1e-30
PAD_SCORE = -3e38

NSA_HEADS = 8
NSA_GROUPS = 2
NSA_HPG = NSA_HEADS // NSA_GROUPS
CMP_BLOCK = 32
CMP_STRIDE = 16
SEL_BLOCK = 64
SEL_TOPK = 16
FORCED_SCORE = 1e4
WINDOW = 512
FOX_HEADS = 8

PEER_HEADS = 8
PEER_N_KEYS = 128
PEER_HALF = 128
PEER_TOPK = 16
PEER_SLOTS = PEER_HEADS * PEER_TOPK
PEER_PARTS = 2

VMEM_LIMIT_BYTES = 56 * 1024 * 1024


def _cparams(sem):
    return pltpu.CompilerParams(dimension_semantics=sem, vmem_limit_bytes=VMEM_LIMIT_BYTES)


def _split3(x):
    hi = x.astype(BF16)
    r = x - hi.astype(F32)
    mid = r.astype(BF16)
    lo = (r - mid.astype(F32)).astype(BF16)
    return hi, mid, lo


def _dot(a, b):
    return jnp.dot(a, b, preferred_element_type=F32)


def _dot_nt(a, b):
    return lax.dot_general(a, b, (((1,), (1,)), ((), ())), preferred_element_type=F32)


def _dot3_right(x, m01):
    hi, mid, lo = _split3(x)
    return _dot(hi, m01) + _dot(mid, m01) + _dot(lo, m01)


def _dot3_left(m01, x):
    hi, mid, lo = _split3(x)
    return _dot(m01, hi) + _dot(m01, mid) + _dot(m01, lo)


def _masked_softmax(s, mask):
    s = jnp.where(mask, s, NEG_INF)
    m = jnp.max(s, axis=-1, keepdims=True)
    e = jnp.where(mask, jnp.exp(s - m), 0.0)
    return e / jnp.maximum(jnp.sum(e, axis=-1, keepdims=True), TINY)


def _rope_tile(x, cos_full, sin_signed):
    return x * cos_full + pltpu.roll(x, HEAD_DIM // 2, axis=1) * sin_signed


def _rmsnorm_kernel(x_ref, g_ref, *o_refs):
    x = x_ref[...]
    y = x * lax.rsqrt(jnp.mean(x * x, axis=-1, keepdims=True) + NORM_EPS) * g_ref[...]
    for o_ref in o_refs:
        o_ref[...] = y.astype(o_ref.dtype)


def _rmsnorm(x2, gain, out_dtypes, tm=512):
    T, D = x2.shape
    outs = pl.pallas_call(
        _rmsnorm_kernel,
        grid=(T // tm,),
        in_specs=[pl.BlockSpec((tm, D), lambda i: (i, 0)), pl.BlockSpec((1, D), lambda i: (0, 0))],
        out_specs=[pl.BlockSpec((tm, D), lambda i: (i, 0)) for _ in out_dtypes],
        out_shape=[jax.ShapeDtypeStruct((T, D), dt) for dt in out_dtypes],
        compiler_params=_cparams(("parallel",)),
        name="rmsnorm",
    )(x2, gain.reshape(1, D).astype(F32))
    return outs


def _mm_kernel(a_ref, b_ref, *rest, epilogue):
    o_ref = rest[-1]
    acc = _dot(a_ref[...], b_ref[...])
    if epilogue == "rope":
        cos_ref, sin_ref = rest[0], rest[1]
        c, s = cos_ref[...], sin_ref[...]
        for j in range(acc.shape[1] // HEAD_DIM):
            sl = slice(j * HEAD_DIM, (j + 1) * HEAD_DIM)
            o_ref[:, sl] = _rope_tile(acc[:, sl], c, s).astype(o_ref.dtype)
    elif epilogue == "sigmoid_bias":
        o_ref[...] = jax.nn.sigmoid(acc + rest[0][...]).astype(o_ref.dtype)
    elif epilogue == "residual":
        o_ref[...] = (rest[0][...] + acc).astype(o_ref.dtype)
    else:
        o_ref[...] = acc.astype(o_ref.dtype)


def _matmul(a, b, out_dtype, epilogue="none", extras=(), tm=1024, tn=512, seq=None, name="matmul"):
    M, K = a.shape
    _, N = b.shape
    tn = min(tn, N)
    tm = min(tm, M)
    in_specs = [pl.BlockSpec((tm, K), lambda i, j: (i, 0)), pl.BlockSpec((K, tn), lambda i, j: (0, j))]
    if epilogue == "rope":
        nrep = seq // tm
        in_specs += [pl.BlockSpec((tm, HEAD_DIM), lambda i, j: (i % nrep, 0))] * 2
    elif epilogue == "sigmoid_bias":
        in_specs += [pl.BlockSpec((1, tn), lambda i, j: (0, j))]
    elif epilogue == "residual":
        in_specs += [pl.BlockSpec((tm, tn), lambda i, j: (i, j))]
    return pl.pallas_call(
        functools.partial(_mm_kernel, epilogue=epilogue),
        grid=(M // tm, N // tn),
        in_specs=in_specs,
        out_specs=pl.BlockSpec((tm, tn), lambda i, j: (i, j)),
        out_shape=jax.ShapeDtypeStruct((M, N), out_dtype),
        compiler_params=_cparams(("parallel", "parallel")),
        name=name,
    )(a, b, *extras)


def _compress_kernel(ch_ref, pos_ref, w1t_ref, w1b_ref, w2_ref, cos_ref, sin_ref, o_ref, *, rope):
    ch = ch_ref[...].astype(F32)
    a_top = (ch + pos_ref[0:1, :]).astype(BF16)
    a_bot = (ch + pos_ref[1:2, :]).astype(BF16)
    y_top = _dot(a_top, w1t_ref[...])
    y_bot = _dot(a_bot, w1b_ref[...])
    n = y_bot.shape[0]
    hidden = y_top + pltpu.roll(y_bot, n - 1, axis=0)
    out = _dot(jax.nn.gelu(hidden).astype(BF16), w2_ref[...])
    if rope:
        out = _rope_tile(out, cos_ref[...], sin_ref[...])
    row = lax.broadcasted_iota(I32, out.shape, 0)
    o_ref[...] = jnp.where(row < n - 1, out, 0.0).astype(o_ref.dtype)


def _compress(chunks, pos, w1, w2, cos_c, sin_c, rope):
    BG, NC, CK = chunks.shape
    hid = w1.shape[1]
    pos2 = pos.reshape(2, CK).astype(F32)
    return pl.pallas_call(
        functools.partial(_compress_kernel, rope=rope),
        grid=(BG,),
        in_specs=[
            pl.BlockSpec((None, NC, CK), lambda i: (i, 0, 0)),
            pl.BlockSpec((2, CK), lambda i: (0, 0)),
            pl.BlockSpec((CK, hid), lambda i: (0, 0)),
            pl.BlockSpec((CK, hid), lambda i: (1, 0)),
            pl.BlockSpec((hid, HEAD_DIM), lambda i: (0, 0)),
            pl.BlockSpec((NC, HEAD_DIM), lambda i: (0, 0)),
            pl.BlockSpec((NC, HEAD_DIM), lambda i: (0, 0)),
        ],
        out_specs=pl.BlockSpec((None, NC, HEAD_DIM), lambda i: (i, 0, 0)),
        out_shape=jax.ShapeDtypeStruct((BG, NC, HEAD_DIM), BF16),
        compiler_params=_cparams(("parallel",)),
        name="compress",
    )(chunks, pos2, w1.astype(BF16), w1.astype(BF16), w2.astype(BF16), cos_c, sin_c)


CAUSAL_SPANS = 4


def _causal_span(q_end, seq):
    return lax.shift_right_logical(q_end - 1, (seq // CAUSAL_SPANS).bit_length() - 1)


def _nsa_kernel(q_ref, ks_ref, kw_ref, vs_ref, vw_ref, kc_ref, vc_ref, gate_ref, gbias_ref, ovl_ref, exp_ref,
                o_ref, osel_scr, *, tq, seq):
    i = pl.program_id(1)
    t0 = i * tq
    scale = HEAD_DIM ** -0.5
    n_sel = seq // SEL_BLOCK
    n_cmp = seq // CMP_STRIDE - CMP_BLOCK // CMP_STRIDE + 1
    wlen = tq + WINDOW

    t_col = t0 + lax.broadcasted_iota(I32, (tq, 1), 0)
    lane = lax.broadcasted_iota(I32, (tq, LANES), 1)
    gate = jax.nn.sigmoid(gate_ref[...] + gbias_ref[...])

    cmp_mask = (lane * CMP_STRIDE + (CMP_BLOCK - 1) <= t_col) & (lane < n_cmp)
    key_all = lax.broadcasted_iota(I32, (tq, seq), 1)
    causal = key_all <= t_col
    wstart = pl.multiple_of(jnp.maximum(t0 - WINDOW, 0), tq)
    key_w = wstart + lax.broadcasted_iota(I32, (tq, wlen), 1)
    win_mask = (key_w <= t_col) & (t_col - key_w < WINDOW)

    cur = t_col >> 6
    forced = (lane == 0) | (lane == cur) | (lane == cur - 1)
    future = lane * SEL_BLOCK > t_col

    for g in range(NSA_GROUPS):
        gs = slice(g * HEAD_DIM, (g + 1) * HEAD_DIM)
        kc = kc_ref[g]
        vc = vc_ref[g]
        o_cmp = []
        psum = jnp.zeros((tq, LANES), F32)
        for h in range(NSA_HPG):
            hs = slice((g * NSA_HPG + h) * HEAD_DIM, (g * NSA_HPG + h + 1) * HEAD_DIM)
            p = _masked_softmax(_dot_nt(q_ref[:, hs], kc) * scale, cmp_mask)
            psum = psum + p
            o_cmp.append(_dot(p.astype(BF16), vc))
        imp = _dot3_right(psum, ovl_ref[...])
        score = jnp.where(forced, FORCED_SCORE, imp)
        score = jnp.where(future, NEG_INF, score)
        score = jnp.where(lane < n_sel, score, PAD_SCORE)
        rank = jnp.zeros((tq, LANES), I32)
        for k in range(n_sel):
            ck = score[:, k:k + 1]
            beats = (ck > score) | ((ck == score) & (lane > k))
            rank = rank + beats.astype(I32)
        sel = ((rank < min(SEL_TOPK, n_sel)) & (lane < n_sel)).astype(BF16)

        for span in range(CAUSAL_SPANS):
            klen = (span + 1) * (seq // CAUSAL_SPANS)

            @pl.when(_causal_span(t0 + tq, seq) == span)
            def _(klen=klen):
                sel_mask = (_dot(sel, exp_ref[:, 0:klen]) > 0.5) & causal[:, 0:klen]
                for h in range(NSA_HPG):
                    hh = g * NSA_HPG + h
                    hs = slice(hh * HEAD_DIM, (hh + 1) * HEAD_DIM)
                    p = _masked_softmax(_dot_nt(q_ref[:, hs], ks_ref[0:klen, gs]) * scale, sel_mask)
                    gl = hh * 3 + 1
                    osel_scr[:, hs] = gate[:, gl:gl + 1] * _dot(p.astype(BF16), vs_ref[0:klen, gs])

        kw = kw_ref[pl.ds(wstart, wlen), gs]
        vw = vw_ref[pl.ds(wstart, wlen), gs]
        for h in range(NSA_HPG):
            hh = g * NSA_HPG + h
            hs = slice(hh * HEAD_DIM, (hh + 1) * HEAD_DIM)
            p = _masked_softmax(_dot_nt(q_ref[:, hs], kw) * scale, win_mask)
            o_win = _dot(p.astype(BF16), vw)
            gl = hh * 3
            o = gate[:, gl:gl + 1] * o_cmp[h] + osel_scr[:, hs] + gate[:, gl + 2:gl + 3] * o_win
            o_ref[:, hs] = o.astype(o_ref.dtype)


def _nsa(rope_proj, plain_proj, kc, vc, small_proj, gate_bias, B, S, tq=256):
    n_sel = S // SEL_BLOCK
    n_cmp = S // CMP_STRIDE - CMP_BLOCK // CMP_STRIDE + 1
    c_start = np.arange(LANES) * CMP_STRIDE
    s_start = np.arange(LANES) * SEL_BLOCK
    ovl = (c_start[:, None] < s_start[None, :] + SEL_BLOCK) & (s_start[None, :] < c_start[:, None] + CMP_BLOCK)
    ovl &= (np.arange(LANES)[:, None] < n_cmp) & (np.arange(LANES)[None, :] < n_sel)
    expand = (np.arange(LANES)[:, None] == (np.arange(S)[None, :] // SEL_BLOCK))
    rp = rope_proj.reshape(B, S, -1)
    pp = plain_proj.reshape(B, S, -1)
    sm = small_proj.reshape(B, S, LANES)
    gw = NSA_GROUPS * HEAD_DIM
    qw = NSA_HEADS * HEAD_DIM
    return pl.pallas_call(
        functools.partial(_nsa_kernel, tq=tq, seq=S),
        grid=(B, S // tq),
        in_specs=[
            pl.BlockSpec((None, tq, qw), lambda b, i: (b, i, 0)),
            pl.BlockSpec((None, S, gw), lambda b, i: (b, 0, qw // gw)),
            pl.BlockSpec((None, S, gw), lambda b, i: (b, 0, qw // gw + 1)),
            pl.BlockSpec((None, S, gw), lambda b, i: (b, 0, 2)),
            pl.BlockSpec((None, S, gw), lambda b, i: (b, 0, 3)),
            pl.BlockSpec((None, NSA_GROUPS, LANES, HEAD_DIM), lambda b, i: (b, 0, 0, 0)),
            pl.BlockSpec((None, NSA_GROUPS, LANES, HEAD_DIM), lambda b, i: (b, 0, 0, 0)),
            pl.BlockSpec((None, tq, LANES), lambda b, i: (b, i, 0)),
            pl.BlockSpec((1, LANES), lambda b, i: (0, 0)),
            pl.BlockSpec((LANES, LANES), lambda b, i: (0, 0)),
            pl.BlockSpec((LANES, S), lambda b, i: (0, 0)),
        ],
        out_specs=pl.BlockSpec((None, tq, qw), lambda b, i: (b, i, 0)),
        out_shape=jax.ShapeDtypeStruct((B, S, qw), BF16),
        scratch_shapes=[pltpu.VMEM((tq, qw), F32)],
        compiler_params=_cparams(("parallel", "parallel")),
        name="nsa",
    )(rp, rp, rp, pp, pp, kc, vc, sm, gate_bias, jnp.asarray(ovl, BF16), jnp.asarray(expand, BF16))


def _forget_cumsum_kernel(sm_ref, fbias_ref, tri_ref, ccol_ref, crow_ref, *, seq, lane0):
    x = sm_ref[...] + fbias_ref[...]
    lane = lax.broadcasted_iota(I32, x.shape, 1)
    log_f = jnp.minimum(x, 0.0) - jnp.log1p(jnp.exp(-jnp.abs(x)))
    log_f = jnp.where((lane >= lane0) & (lane < lane0 + FOX_HEADS), log_f, 0.0)
    carry = jnp.zeros((1, LANES), F32)
    blk = tri_ref.shape[0]
    for r in range(seq // blk):
        c = _dot3_left(tri_ref[...], log_f[r * blk:(r + 1) * blk, :]) + carry
        ccol_ref[r * blk:(r + 1) * blk, :] = c
        carry = c[blk - 1:blk, :]
    crow_ref[...] = ccol_ref[...].T[lane0:lane0 + FOX_HEADS, :]


def _forget_cumsum(small_proj, fbias, B, S, lane0):
    tri = np.tril(np.ones((LANES, LANES), np.float32))
    return pl.pallas_call(
        functools.partial(_forget_cumsum_kernel, seq=S, lane0=lane0),
        grid=(B,),
        in_specs=[
            pl.BlockSpec((None, S, LANES), lambda b: (b, 0, 0)),
            pl.BlockSpec((1, LANES), lambda b: (0, 0)),
            pl.BlockSpec((LANES, LANES), lambda b: (0, 0)),
        ],
        out_specs=[
            pl.BlockSpec((None, S, LANES), lambda b: (b, 0, 0)),
            pl.BlockSpec((None, FOX_HEADS, S), lambda b: (b, 0, 0)),
        ],
        out_shape=[jax.ShapeDtypeStruct((B, S, LANES), F32), jax.ShapeDtypeStruct((B, FOX_HEADS, S), F32)],
        compiler_params=_cparams(("parallel",)),
        name="forget_cumsum",
    )(small_proj.reshape(B, S, LANES), fbias, jnp.asarray(tri, BF16))


def _fox_kernel(q_ref, k_ref, v_ref, ccol_ref, crow_ref, o_ref, *, tq, seq, lane0):
    i = pl.program_id(1)
    t_col = i * tq + lax.broadcasted_iota(I32, (tq, 1), 0)
    scale = HEAD_DIM ** -0.5
    for span in range(CAUSAL_SPANS):
        klen = (span + 1) * (seq // CAUSAL_SPANS)

        @pl.when(_causal_span((i + 1) * tq, seq) == span)
        def _(klen=klen):
            causal = lax.broadcasted_iota(I32, (tq, klen), 1) <= t_col
            for h in range(FOX_HEADS):
                hs = slice(h * HEAD_DIM, (h + 1) * HEAD_DIM)
                s = _dot_nt(q_ref[:, hs], k_ref[0:klen, hs]) * scale
                s = s + ccol_ref[:, lane0 + h:lane0 + h + 1] - crow_ref[h:h + 1, 0:klen]
                p = _masked_softmax(s, causal)
                o_ref[:, hs] = _dot(p.astype(BF16), v_ref[0:klen, hs]).astype(o_ref.dtype)


def _fox(plain_proj, ccol, crow, B, S, lane0, tq=256):
    pp = plain_proj.reshape(B, S, -1)
    fw = FOX_HEADS * HEAD_DIM
    return pl.pallas_call(
        functools.partial(_fox_kernel, tq=tq, seq=S, lane0=lane0),
        grid=(B, S // tq),
        in_specs=[
            pl.BlockSpec((None, tq, fw), lambda b, i: (b, i, 1)),
            pl.BlockSpec((None, S, fw), lambda b, i: (b, 0, 2)),
            pl.BlockSpec((None, S, fw), lambda b, i: (b, 0, 3)),
            pl.BlockSpec((None, tq, LANES), lambda b, i: (b, i, 0)),
            pl.BlockSpec((None, FOX_HEADS, S), lambda b, i: (b, 0, 0)),
        ],
        out_specs=pl.BlockSpec((None, tq, fw), lambda b, i: (b, i, 0)),
        out_shape=jax.ShapeDtypeStruct((B, S, fw), BF16),
        compiler_params=_cparams(("parallel", "parallel")),
        name="fox",
    )(pp, pp, pp, ccol, crow)


def _merge_kernel(on_ref, of_ref, wn_ref, wf_ref, ga_ref, gb_ref, o_ref):
    a = _dot(on_ref[...], wn_ref[...])
    b = _dot(of_ref[...], wf_ref[...])
    o_ref[...] = (ga_ref[...].astype(F32) * a + gb_ref[...].astype(F32) * b).astype(o_ref.dtype)


def _merge(o_nsa, o_fox, w_up_nsa, w_up_fox, gates, d_model, tm=1024, tn=512):
    T, K = o_nsa.shape
    nb = d_model // tn
    return pl.pallas_call(
        _merge_kernel,
        grid=(T // tm, nb),
        in_specs=[
            pl.BlockSpec((tm, K), lambda i, j: (i, 0)),
            pl.BlockSpec((tm, K), lambda i, j: (i, 0)),
            pl.BlockSpec((K, tn), lambda i, j: (0, j)),
            pl.BlockSpec((K, tn), lambda i, j: (0, j)),
            pl.BlockSpec((tm, tn), lambda i, j: (i, j)),
            pl.BlockSpec((tm, tn), lambda i, j: (i, j + nb)),
        ],
        out_specs=pl.BlockSpec((tm, tn), lambda i, j: (i, j)),
        out_shape=jax.ShapeDtypeStruct((T, d_model), BF16),
        compiler_params=_cparams(("parallel", "parallel")),
        name="merge",
    )(o_nsa, o_fox, w_up_nsa, w_up_fox, gates, gates)


def _rope_tables(pos):
    inv_freq = ROPE_THETA ** (-jnp.arange(0, HEAD_DIM, 2, dtype=F32) / HEAD_DIM)
    ang = pos.astype(F32)[:, None] * inv_freq[None, :]
    c, s = jnp.cos(ang), jnp.sin(ang)
    return jnp.concatenate([c, c], axis=-1), jnp.concatenate([-s, s], axis=-1)


def _mixer(x2, B, S, norm_gain, w_in, nsa_gate_bias, fox_forget_bias, merge_gate_bias, k_cmp_pos, k_cmp_w1, k_cmp_w2,
           v_cmp_pos, v_cmp_w1, v_cmp_w2, w_up_nsa, w_up_fox, w_out):
    T, D = x2.shape
    qw = NSA_HEADS * HEAD_DIM
    gw = NSA_GROUPS * HEAD_DIM
    fw = FOX_HEADS * HEAD_DIM
    ngate = 3 * NSA_HEADS
    widths = (qw, gw, gw, gw, gw, gw, gw, ngate, fw, fw, fw, FOX_HEADS, D, D)
    offs = np.concatenate([[0], np.cumsum(widths)])
    (w_qn, w_kc, w_vc, w_ks, w_vs, w_kw, w_vw, w_gn, w_qf, w_kf, w_vf, w_ff, w_ga, w_gb) = [
        w_in[:, offs[j]:offs[j + 1]] for j in range(len(widths))]
    w_rope = jnp.concatenate([w_qn, w_ks, w_kw], axis=1).astype(BF16)
    w_plain = jnp.concatenate([w_kc, w_vc, w_vs, w_vw, w_qf, w_kf, w_vf], axis=1).astype(BF16)
    w_gate = jnp.concatenate([w_ga, w_gb], axis=1).astype(BF16)
    pad = LANES - ngate - FOX_HEADS
    w_small = jnp.concatenate([w_gn, w_ff, jnp.zeros((D, pad), w_in.dtype)], axis=1).astype(BF16)
    gate_bias = jnp.concatenate([nsa_gate_bias, jnp.zeros((LANES - ngate,), F32)]).reshape(1, LANES)
    fbias = jnp.concatenate([jnp.zeros((ngate,), F32), fox_forget_bias, jnp.zeros((pad,), F32)]).reshape(1, LANES)

    (h,) = _rmsnorm(x2, norm_gain, (BF16,))
    cos_t, sin_t = _rope_tables(jnp.arange(S))
    rope_proj = _matmul(h, w_rope, BF16, "rope", (cos_t, sin_t), seq=S, name="inproj_rope")
    plain_proj = _matmul(h, w_plain, BF16, name="inproj_plain")
    gates = _matmul(h, w_gate, BF16, "sigmoid_bias", (merge_gate_bias.reshape(1, 2 * D),), name="inproj_gate")
    small_proj = _matmul(h, w_small, F32, name="inproj_small")

    n_chunks = S // CMP_STRIDE
    ck = CMP_STRIDE * HEAD_DIM

    def chunks(col0):
        c = plain_proj[:, col0:col0 + gw].reshape(B, S, NSA_GROUPS, HEAD_DIM).transpose(0, 2, 1, 3)
        return c.reshape(B * NSA_GROUPS, n_chunks, ck)

    cos_c, sin_c = _rope_tables(jnp.arange(n_chunks) * CMP_STRIDE + CMP_BLOCK - 1)
    kc = _compress(chunks(0), k_cmp_pos, k_cmp_w1, k_cmp_w2, cos_c, sin_c, True)
    vc = _compress(chunks(gw), v_cmp_pos, v_cmp_w1, v_cmp_w2, cos_c, sin_c, False)
    kc = kc.reshape(B, NSA_GROUPS, n_chunks, HEAD_DIM)
    vc = vc.reshape(B, NSA_GROUPS, n_chunks, HEAD_DIM)

    o_nsa = _nsa(rope_proj, plain_proj, kc, vc, small_proj, gate_bias, B, S)
    ccol, crow = _forget_cumsum(small_proj, fbias, B, S, ngate)
    o_fox = _fox(plain_proj, ccol, crow, B, S, ngate)

    merged = _merge(o_nsa.reshape(T, qw), o_fox.reshape(T, fw), w_up_nsa.astype(BF16), w_up_fox.astype(BF16), gates, D)
    return _matmul(merged, w_out.astype(BF16), F32, "residual", (x2,), name="outproj")


def _route_kernel(xn_ref, wq_ref, k1_ref, k2_ref, e_ref, g_ref, *, tm):
    q = _dot(xn_ref[...], wq_ref[...])
    s1 = _dot_nt(k1_ref[...], q[:, :PEER_HALF].astype(BF16))
    s2 = _dot_nt(k2_ref[...], q[:, PEER_HALF:].astype(BF16))
    nc = PEER_TOPK * PEER_TOPK
    key_row = lax.broadcasted_iota(I32, (PEER_N_KEYS, tm), 0).astype(F32)
    rank_row = lax.broadcasted_iota(I32, (PEER_TOPK, tm), 0)
    cand_row = lax.broadcasted_iota(I32, (nc, tm), 0).astype(F32)

    def top_keys(s):
        vals = jnp.zeros((PEER_TOPK, tm), F32)
        idxs = jnp.zeros((PEER_TOPK, tm), F32)
        for r in range(PEER_TOPK):
            m = jnp.max(s, axis=0, keepdims=True)
            idx = jnp.min(jnp.where(s == m, key_row, float(PEER_N_KEYS)), axis=0, keepdims=True)
            vals = jnp.where(rank_row == r, m, vals)
            idxs = jnp.where(rank_row == r, idx, idxs)
            s = jnp.where(key_row == idx, -jnp.inf, s)
        return vals, idxs

    v1, i1 = top_keys(s1)
    v2, i2 = top_keys(s2)
    cand = jnp.concatenate([v1[k:k + 1, :] + v2 for k in range(PEER_TOPK)], axis=0)
    cidx = jnp.concatenate([i1[k:k + 1, :] * float(PEER_N_KEYS) + i2 for k in range(PEER_TOPK)], axis=0)

    top_s = jnp.zeros((PEER_TOPK, tm), F32)
    top_e = jnp.zeros((PEER_TOPK, tm), F32)
    for r in range(PEER_TOPK):
        m = jnp.max(cand, axis=0, keepdims=True)
        pos = jnp.min(jnp.where(cand == m, cand_row, float(nc)), axis=0, keepdims=True)
        hit = cand_row == pos
        e = jnp.sum(jnp.where(hit, cidx, 0.0), axis=0, keepdims=True)
        top_s = jnp.where(rank_row == r, m, top_s)
        top_e = jnp.where(rank_row == r, e, top_e)
        cand = jnp.where(hit, -jnp.inf, cand)

    ex = jnp.exp(top_s - top_s[0:1, :])
    e_ref[...] = top_e.astype(I32)
    g_ref[...] = ex / jnp.sum(ex, axis=0, keepdims=True)


def _route(xn_bf16, w_query, sub_keys_1, sub_keys_2, tm=256):
    T, D = xn_bf16.shape
    kd = 2 * PEER_HALF
    return pl.pallas_call(
        functools.partial(_route_kernel, tm=tm),
        grid=(T // tm, PEER_HEADS),
        in_specs=[
            pl.BlockSpec((tm, D), lambda i, h: (i, 0)),
            pl.BlockSpec((D, kd), lambda i, h: (0, h)),
            pl.BlockSpec((None, PEER_N_KEYS, PEER_HALF), lambda i, h: (h, 0, 0)),
            pl.BlockSpec((None, PEER_N_KEYS, PEER_HALF), lambda i, h: (h, 0, 0)),
        ],
        out_specs=[pl.BlockSpec((PEER_TOPK, tm), lambda i, h: (h, i))] * 2,
        out_shape=[jax.ShapeDtypeStruct((PEER_SLOTS, T), I32), jax.ShapeDtypeStruct((PEER_SLOTS, T), F32)],
        compiler_params=_cparams(("parallel", "parallel")),
        name="peer_route",
    )(xn_bf16, w_query.astype(BF16), sub_keys_1.astype(BF16), sub_keys_2.astype(BF16))


def _slot_sort_kernel(e_ref, g_ref, se_ref, sg_ref, meta_ref, *, tm, part_shift):
    row = lax.broadcasted_iota(I32, (PEER_SLOTS, tm), 0)
    e = e_ref[...]
    g = g_ref[...]
    key = e * PEER_SLOTS + row
    rank = jnp.zeros((PEER_SLOTS, tm), I32)
    for k in range(PEER_SLOTS):
        rank = rank + (key[k:k + 1, :] < key).astype(I32)
    se = jnp.zeros((PEER_SLOTS, tm), I32)
    sg = jnp.zeros((PEER_SLOTS, tm), F32)
    for k in range(PEER_SLOTS):
        hit = rank[k:k + 1, :] == row
        se = jnp.where(hit, e[k:k + 1, :], se)
        sg = jnp.where(hit, g[k:k + 1, :], sg)
    part = se >> part_shift
    meta = jnp.zeros((PEER_SLOTS, tm), I32)
    start = jnp.zeros((1, tm), I32)
    for p in range(PEER_PARTS):
        cnt = jnp.sum((part == p).astype(I32), axis=0, keepdims=True)
        meta = jnp.where(row == p, start, meta)
        meta = jnp.where(row == PEER_PARTS + p, cnt, meta)
        start = start + cnt
    se_ref[...] = (se & ((1 << part_shift) - 1)).T
    sg_ref[...] = sg.T
    meta_ref[...] = meta.T


def _slot_sort(e, g, part_shift, tm=128):
    T = e.shape[1]
    in_spec = pl.BlockSpec((PEER_SLOTS, tm), lambda i: (0, i))
    out_spec = pl.BlockSpec((tm, PEER_SLOTS), lambda i: (i, 0))
    return pl.pallas_call(
        functools.partial(_slot_sort_kernel, tm=tm, part_shift=part_shift),
        grid=(T // tm,),
        in_specs=[in_spec, in_spec],
        out_specs=[out_spec, out_spec, out_spec],
        out_shape=[jax.ShapeDtypeStruct((T, PEER_SLOTS), I32), jax.ShapeDtypeStruct((T, PEER_SLOTS), F32),
                   jax.ShapeDtypeStruct((T, PEER_SLOTS), I32)],
        compiler_params=_cparams(("parallel",)),
        name="peer_slot_sort",
    )(e, g)


SUBLANES = 8
META_W = 2 * PEER_PARTS
_BITREV3 = (0, 4, 2, 6, 1, 5, 3, 7)


GROUP_TILES = 2
GROUP_SLOTS = GROUP_TILES * SUBLANES
_GROUP_SHIFT = GROUP_SLOTS.bit_length() - 1
_LAST_GROUP = PEER_SLOTS // GROUP_SLOTS - 1


def _slot_groups(meta_ref, t, p):
    start = meta_ref[t * META_W + p]
    cnt = meta_ref[t * META_W + PEER_PARTS + p]
    g0 = lax.shift_right_logical(start, _GROUP_SHIFT)
    g1 = jnp.where(cnt > 0, lax.shift_right_logical(start + cnt + (GROUP_SLOTS - 1), _GROUP_SHIFT), g0)
    return start, cnt, g0, g1


def _gather_rows(tab_ref, se_ref, t, g, order):
    base = t * PEER_SLOTS + g * GROUP_SLOTS
    return tuple(tab_ref[se_ref[base + tile * SUBLANES + i]] for tile in range(GROUP_TILES) for i in order)


def _fold_rows(a, b, first, sh):
    return jnp.where(first, a + pltpu.roll(a, SUBLANES - sh, axis=0), b + pltpu.roll(b, sh, axis=0))


def _peer_down_kernel(u_ref, x_ref, se_ref, meta_ref, sg_ref, w_ref, r_scr, a_scr, *, tb):
    p = pl.program_id(0)
    sub = lax.broadcasted_iota(I32, (SUBLANES, LANES), 0)
    m4, m2, m1 = sub < 4, (sub & 2) == 0, (sub & 1) == 0
    lane1 = lax.broadcasted_iota(I32, (1, LANES), 1)

    @pl.when((p == 0) & (pl.program_id(1) == 0))
    def _():
        r_scr[...] = jnp.zeros_like(r_scr)

    def token(t, carry):
        _, _, g0, g1 = _slot_groups(meta_ref, t, p)
        xt = x_ref[t]

        def fold_group(g, rows):
            base = pl.multiple_of(t * PEER_SLOTS + g * GROUP_SLOTS, GROUP_SLOTS)
            for tile in range(GROUP_TILES):
                parts = []
                for k in range(SUBLANES):
                    pr = xt * rows[tile * SUBLANES + k].astype(F32)
                    parts.append(pr[0:8] + pr[8:16])
                q = [_fold_rows(parts[2 * m], parts[2 * m + 1], m4, 4) for m in range(4)]
                h = [_fold_rows(q[0], q[1], m2, 2), _fold_rows(q[2], q[3], m2, 2)]
                r_scr[pl.ds(base + tile * SUBLANES, SUBLANES), :] = _fold_rows(h[0], h[1], m1, 1)

        def group(g, rows):
            nxt = _gather_rows(u_ref, se_ref, t, jnp.minimum(g + 1, _LAST_GROUP), _BITREV3)
            fold_group(g, rows)
            return nxt

        lax.fori_loop(g0, g1, group, _gather_rows(u_ref, se_ref, t, jnp.minimum(g0, _LAST_GROUP), _BITREV3))
        return carry

    lax.fori_loop(0, tb, token, 0)

    def finish(b, carry):
        for tl in range(SUBLANES):
            t = b * SUBLANES + tl
            start = meta_ref[t * META_W + p]
            cnt = meta_ref[t * META_W + PEER_PARTS + p]
            r_t = r_scr[pl.ds(pl.multiple_of(t * PEER_SLOTS, PEER_SLOTS), PEER_SLOTS), :]
            a_row = jnp.sum(r_t.T, axis=0, keepdims=True)
            valid = (lane1 >= start) & (lane1 < start + cnt)
            a_scr[pl.ds(t, 1), :] = jnp.where(valid, a_row, 0.0)
        return carry

    lax.fori_loop(0, tb // SUBLANES, finish, 0)
    w_ref[...] = jax.nn.gelu(a_scr[...]) * sg_ref[...]


def _peer_up_kernel(v_ref, y_ref, se_ref, meta_ref, w_ref, o_ref, *, tb):
    p = pl.program_id(0)
    order = tuple(range(SUBLANES))

    def token(t, carry):
        _, _, g0, g1 = _slot_groups(meta_ref, t, p)

        def group(g, acc):
            rows = _gather_rows(v_ref, se_ref, t, g, order)
            base = t * PEER_SLOTS + g * GROUP_SLOTS
            for i in range(GROUP_SLOTS):
                acc = acc + w_ref[base + i] * rows[i].astype(F32)
            return acc

        o_ref[t] = lax.fori_loop(g0, g1, group, y_ref[t])
        return carry

    lax.fori_loop(0, tb, token, 0)


def _peer_experts(xn3, resid3, se, sg, meta, down_tab, up_tab, tb=64):
    T = xn3.shape[0]
    E = down_tab.shape[0]
    part_size = E // PEER_PARTS
    nt = T // tb
    rows = xn3.shape[1]
    se_flat = se.reshape(T * PEER_SLOTS)
    meta_flat = meta[:, :META_W].reshape(T * META_W)
    tab_spec = pl.BlockSpec((part_size, rows, LANES), lambda p, i: (p, 0, 0), pipeline_mode=pl.Buffered(1))
    tok_spec = pl.BlockSpec((tb, rows, LANES), lambda p, i: (i, 0, 0))
    se_spec = pl.BlockSpec((tb * PEER_SLOTS,), lambda p, i: (i,), memory_space=pltpu.SMEM)
    meta_spec = pl.BlockSpec((tb * META_W,), lambda p, i: (i,), memory_space=pltpu.SMEM)
    w_parts = pl.pallas_call(
        functools.partial(_peer_down_kernel, tb=tb),
        grid=(PEER_PARTS, nt),
        in_specs=[tab_spec, tok_spec, se_spec, meta_spec, pl.BlockSpec((tb, PEER_SLOTS), lambda p, i: (i, 0))],
        out_specs=pl.BlockSpec((None, tb, PEER_SLOTS), lambda p, i: (p, i, 0)),
        out_shape=jax.ShapeDtypeStruct((PEER_PARTS, T, PEER_SLOTS), F32),
        scratch_shapes=[pltpu.VMEM((tb * PEER_SLOTS, LANES), F32), pltpu.VMEM((tb, PEER_SLOTS), F32)],
        compiler_params=_cparams(("arbitrary", "arbitrary")),
        name="peer_down",
    )(down_tab, xn3, se_flat, meta_flat, sg)
    w_flat = w_parts.reshape(PEER_PARTS * T * PEER_SLOTS)
    w_spec = pl.BlockSpec((tb * PEER_SLOTS,), lambda p, i: (p * nt + i,), memory_space=pltpu.SMEM)
    return pl.pallas_call(
        functools.partial(_peer_up_kernel, tb=tb),
        grid=(PEER_PARTS, nt),
        in_specs=[tab_spec, tok_spec, se_spec, meta_spec, w_spec],
        out_specs=tok_spec,
        out_shape=jax.ShapeDtypeStruct(resid3.shape, F32),
        input_output_aliases={1: 0},
        compiler_params=_cparams(("arbitrary", "arbitrary")),
        name="peer_up",
    )(up_tab, resid3, se_flat, meta_flat, w_flat)


def _peer(x2, norm_gain, w_query, sub_keys_1, sub_keys_2, expert_down, expert_up):
    T, D = x2.shape
    E = expert_down.shape[0]
    rows = D // LANES
    xn_bf16, xn = _rmsnorm(x2, norm_gain, (BF16, F32))
    e, g = _route(xn_bf16, w_query, sub_keys_1, sub_keys_2)
    part_shift = int(math.log2(E // PEER_PARTS))
    se, sg, meta = _slot_sort(e, g, part_shift)
    down_tab = expert_down.astype(BF16).reshape(E, rows, LANES)
    up_tab = expert_up.astype(BF16).reshape(E, rows, LANES)
    y = _peer_experts(xn.reshape(T, rows, LANES), x2.reshape(T, rows, LANES), se, sg, meta, down_tab, up_tab)
    return y.reshape(T, D)


def kernel(x, norm_mix_gain, w_in, nsa_gate_bias, fox_forget_bias, merge_gate_bias, k_cmp_pos, k_cmp_w1, k_cmp_w2, v_cmp_pos, v_cmp_w1, v_cmp_w2, w_up_nsa, w_up_fox, w_out, norm_ffn_gain, peer_w_query, peer_sub_keys_1, peer_sub_keys_2, peer_expert_down, peer_expert_up, norm_final_gain):
    B, S, D = x.shape
    x2 = x.reshape(B * S, D)
    for l in range(norm_mix_gain.shape[0]):
        x2 = _mixer(x2, B, S, norm_mix_gain[l], w_in[l], nsa_gate_bias[l], fox_forget_bias[l], merge_gate_bias[l],
                    k_cmp_pos[l], k_cmp_w1[l], k_cmp_w2[l], v_cmp_pos[l], v_cmp_w1[l], v_cmp_w2[l],
                    w_up_nsa[l], w_up_fox[l], w_out[l])
        x2 = _peer(x2, norm_ffn_gain[l], peer_w_query[l], peer_sub_keys_1[l], peer_sub_keys_2[l], peer_expert_down[l],
                   peer_expert_up[l])
    (out,) = _rmsnorm(x2, norm_final_gain, (F32,))
    return out.reshape(B, S, D)
```

```python
import functools
import math

import jax
import jax.numpy as jnp
import numpy as np
from jax import lax
from jax.experimental import pallas as pl
from jax.experimental.pallas import tpu as pltpu

F32 = jnp.float32
BF16 = jnp.bfloat16
I32 = jnp.int32

LANES = 128
HEAD_DIM = 128
ROPE_THETA = 10000.0
NORM_EPS = 1e-6
NEG_INF = -1e30
TINY = 1e-30
PAD_SCORE = -3e38

NSA_HEADS = 8
NSA_GROUPS = 2
NSA_HPG = NSA_HEADS // NSA_GROUPS
CMP_BLOCK = 32
CMP_STRIDE = 16
SEL_BLOCK = 64
SEL_TOPK = 16
FORCED_SCORE = 1e4
WINDOW = 512
FOX_HEADS = 8

PEER_HEADS = 8
PEER_N_KEYS = 128
PEER_HALF = 128
PEER_TOPK = 16
PEER_SLOTS = PEER_HEADS * PEER_TOPK
PEER_PARTS = 2

VMEM_LIMIT_BYTES = 56 * 1024 * 1024


def _cparams(sem):
    return pltpu.CompilerParams(dimension_semantics=sem, vmem_limit_bytes=VMEM_LIMIT_BYTES)


def _split3(x):
    hi = x.astype(BF16)
    r = x - hi.astype(F32)
    mid = r.astype(BF16)
    lo = (r - mid.astype(F32)).astype(BF16)
    return hi, mid, lo


def _dot(a, b):
    return jnp.dot(a, b, preferred_element_type=F32)


def _dot_nt(a, b):
    return lax.dot_general(a, b, (((1,), (1,)), ((), ())), preferred_element_type=F32)


def _dot3_right(x, m01):
    hi, mid, lo = _split3(x)
    return _dot(hi, m01) + _dot(mid, m01) + _dot(lo, m01)


def _dot3_left(m01, x):
    hi, mid, lo = _split3(x)
    return _dot(m01, hi) + _dot(m01, mid) + _dot(m01, lo)


def _masked_softmax(s, mask):
    s = jnp.where(mask, s, NEG_INF)
    m = jnp.max(s, axis=-1, keepdims=True)
    e = jnp.where(mask, jnp.exp(s - m), 0.0)
    return e / jnp.maximum(jnp.sum(e, axis=-1, keepdims=True), TINY)


def _rope_tile(x, cos_full, sin_signed):
    return x * cos_full + pltpu.roll(x, HEAD_DIM // 2, axis=1) * sin_signed


def _rmsnorm_kernel(x_ref, g_ref, *o_refs):
    x = x_ref[...]
    y = x * lax.rsqrt(jnp.mean(x * x, axis=-1, keepdims=True) + NORM_EPS) * g_ref[...]
    for o_ref in o_refs:
        o_ref[...] = y.astype(o_ref.dtype)


def _rmsnorm(x2, gain, out_dtypes, tm=512):
    T, D = x2.shape
    outs = pl.pallas_call(
        _rmsnorm_kernel,
        grid=(T // tm,),
        in_specs=[pl.BlockSpec((tm, D), lambda i: (i, 0)), pl.BlockSpec((1, D), lambda i: (0, 0))],
        out_specs=[pl.BlockSpec((tm, D), lambda i: (i, 0)) for _ in out_dtypes],
        out_shape=[jax.ShapeDtypeStruct((T, D), dt) for dt in out_dtypes],
        compiler_params=_cparams(("parallel",)),
        name="rmsnorm",
    )(x2, gain.reshape(1, D).astype(F32))
    return outs


def _mm_kernel(a_ref, b_ref, *rest, epilogue, scaled_blocks):
    o_ref = rest[-1]
    acc = _dot(a_ref[...], b_ref[...])
    if scaled_blocks is not None:
        j = pl.program_id(1)
        acc = acc * jnp.where((j >= scaled_blocks[0]) & (j < scaled_blocks[1]), HEAD_DIM ** -0.5, 1.0)
    if epilogue == "rope":
        cos_ref, sin_ref = rest[0], rest[1]
        c, s = cos_ref[...], sin_ref[...]
        for j in range(acc.shape[1] // HEAD_DIM):
            sl = slice(j * HEAD_DIM, (j + 1) * HEAD_DIM)
            o_ref[:, sl] = _rope_tile(acc[:, sl], c, s).astype(o_ref.dtype)
    elif epilogue == "sigmoid_bias":
        o_ref[...] = jax.nn.sigmoid(acc + rest[0][...]).astype(o_ref.dtype)
    elif epilogue == "residual":
        o_ref[...] = (rest[0][...] + acc).astype(o_ref.dtype)
    else:
        o_ref[...] = acc.astype(o_ref.dtype)


def _matmul(a, b, out_dtype, epilogue="none", extras=(), tm=1024, tn=512, seq=None, scaled_cols=None, name="matmul"):
    M, K = a.shape
    _, N = b.shape
    tn = min(tn, N)
    tm = min(tm, M)
    scaled_blocks = None if scaled_cols is None else (scaled_cols[0] // tn, scaled_cols[1] // tn)
    in_specs = [pl.BlockSpec((tm, K), lambda i, j: (i, 0)), pl.BlockSpec((K, tn), lambda i, j: (0, j))]
    if epilogue == "rope":
        nrep = seq // tm
        in_specs += [pl.BlockSpec((tm, HEAD_DIM), lambda i, j: (i % nrep, 0))] * 2
    elif epilogue == "sigmoid_bias":
        in_specs += [pl.BlockSpec((1, tn), lambda i, j: (0, j))]
    elif epilogue == "residual":
        in_specs += [pl.BlockSpec((tm, tn), lambda i, j: (i, j))]
    return pl.pallas_call(
        functools.partial(_mm_kernel, epilogue=epilogue, scaled_blocks=scaled_blocks),
        grid=(M // tm, N // tn),
        in_specs=in_specs,
        out_specs=pl.BlockSpec((tm, tn), lambda i, j: (i, j)),
        out_shape=jax.ShapeDtypeStruct((M, N), out_dtype),
        compiler_params=_cparams(("parallel", "parallel")),
        name=name,
    )(a, b, *extras)


def _compress_kernel(ch_ref, pos_ref, w1t_ref, w1b_ref, w2_ref, cos_ref, sin_ref, o_ref, *, rope):
    ch = ch_ref[...].astype(F32)
    a_top = (ch + pos_ref[0:1, :]).astype(BF16)
    a_bot = (ch + pos_ref[1:2, :]).astype(BF16)
    y_top = _dot(a_top, w1t_ref[...])
    y_bot = _dot(a_bot, w1b_ref[...])
    n = y_bot.shape[0]
    hidden = y_top + pltpu.roll(y_bot, n - 1, axis=0)
    out = _dot(jax.nn.gelu(hidden).astype(BF16), w2_ref[...])
    if rope:
        out = _rope_tile(out, cos_ref[...], sin_ref[...])
    row = lax.broadcasted_iota(I32, out.shape, 0)
    o_ref[...] = jnp.where(row < n - 1, out, 0.0).astype(o_ref.dtype)


def _compress(chunks, pos, w1, w2, cos_c, sin_c, rope):
    BG, NC, CK = chunks.shape
    hid = w1.shape[1]
    pos2 = pos.reshape(2, CK).astype(F32)
    return pl.pallas_call(
        functools.partial(_compress_kernel, rope=rope),
        grid=(BG,),
        in_specs=[
            pl.BlockSpec((None, NC, CK), lambda i: (i, 0, 0)),
            pl.BlockSpec((2, CK), lambda i: (0, 0)),
            pl.BlockSpec((CK, hid), lambda i: (0, 0)),
            pl.BlockSpec((CK, hid), lambda i: (1, 0)),
            pl.BlockSpec((hid, HEAD_DIM), lambda i: (0, 0)),
            pl.BlockSpec((NC, HEAD_DIM), lambda i: (0, 0)),
            pl.BlockSpec((NC, HEAD_DIM), lambda i: (0, 0)),
        ],
        out_specs=pl.BlockSpec((None, NC, HEAD_DIM), lambda i: (i, 0, 0)),
        out_shape=jax.ShapeDtypeStruct((BG, NC, HEAD_DIM), BF16),
        compiler_params=_cparams(("parallel",)),
        name="compress",
    )(chunks, pos2, w1.astype(BF16), w1.astype(BF16), w2.astype(BF16), cos_c, sin_c)


CAUSAL_SPANS = 4


def _causal_span(q_end, seq):
    return lax.shift_right_logical(q_end - 1, (seq // CAUSAL_SPANS).bit_length() - 1)


def _attend(q, k, v, mask):
    s = jnp.where(mask, _dot_nt(q, k), NEG_INF)
    e = jnp.exp(s - jnp.max(s, axis=-1, keepdims=True))
    return _dot(e.astype(BF16), v) * (1.0 / jnp.sum(e, axis=-1, keepdims=True))


def _nsa_kernel(q_ref, ks_ref, kw_ref, vs_ref, vw_ref, kc_ref, vc_ref, gate_ref, gbias_ref, ovl_ref, exp_ref,
                o_ref, osel_scr, *, tq, seq):
    g = pl.program_id(1)
    t0 = pl.program_id(2) * tq
    n_sel = seq // SEL_BLOCK
    n_cmp = seq // CMP_STRIDE - CMP_BLOCK // CMP_STRIDE + 1
    wlen = tq + WINDOW

    t_col = t0 + lax.broadcasted_iota(I32, (tq, 1), 0)
    lane = lax.broadcasted_iota(I32, (tq, LANES), 1)
    gate_all = jax.nn.sigmoid(gate_ref[...] + gbias_ref[...])

    def gate(h, j):
        cols = [gate_all[:, (gg * NSA_HPG + h) * 3 + j:(gg * NSA_HPG + h) * 3 + j + 1] for gg in range(NSA_GROUPS)]
        out = cols[0]
        for gg in range(1, NSA_GROUPS):
            out = jnp.where(g == gg, cols[gg], out)
        return out

    cmp_mask = (lane * CMP_STRIDE + (CMP_BLOCK - 1) <= t_col) & (lane < n_cmp)
    wstart = pl.multiple_of(jnp.maximum(t0 - WINDOW, 0), tq)
    key_w = wstart + lax.broadcasted_iota(I32, (tq, wlen), 1)
    win_mask = (key_w <= t_col) & (t_col - key_w < WINDOW)

    cur = t_col >> 6
    forced = (lane == 0) | (lane == cur) | (lane == cur - 1)
    future = lane * SEL_BLOCK > t_col

    heads = [slice(h * HEAD_DIM, (h + 1) * HEAD_DIM) for h in range(NSA_HPG)]
    kc = kc_ref[...]
    vc = vc_ref[...]
    o_cmp = []
    psum = jnp.zeros((tq, LANES), F32)
    for h in range(NSA_HPG):
        p = _masked_softmax(_dot_nt(q_ref[:, heads[h]], kc), cmp_mask)
        psum = psum + p
        o_cmp.append(_dot(p.astype(BF16), vc))
    imp = _dot3_right(psum, ovl_ref[...])
    score = jnp.where(forced, FORCED_SCORE, imp)
    score = jnp.where(future, NEG_INF, score)
    score = jnp.where(lane < n_sel, score, PAD_SCORE)
    rank = jnp.zeros((tq, LANES), I32)
    for k in range(n_sel):
        ck = score[:, k:k + 1]
        beats = (ck > score) | ((ck == score) & (lane > k))
        rank = rank + beats.astype(I32)
    sel = ((rank < min(SEL_TOPK, n_sel)) & (lane < n_sel)).astype(BF16)

    for span in range(CAUSAL_SPANS):
        klen = (span + 1) * (seq // CAUSAL_SPANS)

        @pl.when(_causal_span(t0 + tq, seq) == span)
        def _(klen=klen):
            causal = lax.broadcasted_iota(I32, (tq, klen), 1) <= t_col
            sel_mask = (_dot(sel, exp_ref[:, 0:klen]) > 0.5) & causal
            for h in range(NSA_HPG):
                osel_scr[:, heads[h]] = gate(h, 1) * _attend(q_ref[:, heads[h]], ks_ref[0:klen, :], vs_ref[0:klen, :], sel_mask)

    kw = kw_ref[pl.ds(wstart, wlen), :]
    vw = vw_ref[pl.ds(wstart, wlen), :]
    for h in range(NSA_HPG):
        o_win = _attend(q_ref[:, heads[h]], kw, vw, win_mask)
        o = gate(h, 0) * o_cmp[h] + osel_scr[:, heads[h]] + gate(h, 2) * o_win
        o_ref[:, heads[h]] = o.astype(o_ref.dtype)


def _nsa(rope_proj, plain_proj, kc, vc, small_proj, gate_bias, B, S, tq=256):
    n_sel = S // SEL_BLOCK
    n_cmp = S // CMP_STRIDE - CMP_BLOCK // CMP_STRIDE + 1
    c_start = np.arange(LANES) * CMP_STRIDE
    s_start = np.arange(LANES) * SEL_BLOCK
    ovl = (c_start[:, None] < s_start[None, :] + SEL_BLOCK) & (s_start[None, :] < c_start[:, None] + CMP_BLOCK)
    ovl &= (np.arange(LANES)[:, None] < n_cmp) & (np.arange(LANES)[None, :] < n_sel)
    expand = (np.arange(LANES)[:, None] == (np.arange(S)[None, :] // SEL_BLOCK))
    rp = rope_proj.reshape(B, S, -1)
    pp = plain_proj.reshape(B, S, -1)
    sm = small_proj.reshape(B, S, LANES)
    qw = NSA_HEADS * HEAD_DIM
    gqw = NSA_HPG * HEAD_DIM
    nq = qw // HEAD_DIM
    kv = lambda col0: pl.BlockSpec((None, S, HEAD_DIM), lambda b, g, i: (b, 0, col0 + g))
    cmp_spec = pl.BlockSpec((None, None, LANES, HEAD_DIM), lambda b, g, i: (b, g, 0, 0))
    return pl.pallas_call(
        functools.partial(_nsa_kernel, tq=tq, seq=S),
        grid=(B, NSA_GROUPS, S // tq),
        in_specs=[
            pl.BlockSpec((None, tq, gqw), lambda b, g, i: (b, i, g)),
            kv(nq), kv(nq + NSA_GROUPS), kv(2 * NSA_GROUPS), kv(3 * NSA_GROUPS),
            cmp_spec, cmp_spec,
            pl.BlockSpec((None, tq, LANES), lambda b, g, i: (b, i, 0)),
            pl.BlockSpec((1, LANES), lambda b, g, i: (0, 0)),
            pl.BlockSpec((LANES, LANES), lambda b, g, i: (0, 0)),
            pl.BlockSpec((LANES, S), lambda b, g, i: (0, 0)),
        ],
        out_specs=pl.BlockSpec((None, tq, gqw), lambda b, g, i: (b, i, g)),
        out_shape=jax.ShapeDtypeStruct((B, S, qw), BF16),
        scratch_shapes=[pltpu.VMEM((tq, gqw), F32)],
        compiler_params=_cparams(("parallel", "parallel", "parallel")),
        name="nsa",
    )(rp, rp, rp, pp, pp, kc, vc, sm, gate_bias, jnp.asarray(ovl, BF16), jnp.asarray(expand, BF16))


def _forget_cumsum_kernel(sm_ref, fbias_ref, tri_ref, ccol_ref, crow_ref, *, seq, lane0):
    x = sm_ref[...] + fbias_ref[...]
    lane = lax.broadcasted_iota(I32, x.shape, 1)
    log_f = jnp.minimum(x, 0.0) - jnp.log1p(jnp.exp(-jnp.abs(x)))
    log_f = jnp.where((lane >= lane0) & (lane < lane0 + FOX_HEADS), log_f, 0.0)
    carry = jnp.zeros((1, LANES), F32)
    blk = tri_ref.shape[0]
    for r in range(seq // blk):
        c = _dot3_left(tri_ref[...], log_f[r * blk:(r + 1) * blk, :]) + carry
        ccol_ref[r * blk:(r + 1) * blk, :] = c
        carry = c[blk - 1:blk, :]
    crow_ref[...] = ccol_ref[...].T[lane0:lane0 + FOX_HEADS, :]


def _forget_cumsum(small_proj, fbias, B, S, lane0):
    tri = np.tril(np.ones((LANES, LANES), np.float32))
    return pl.pallas_call(
        functools.partial(_forget_cumsum_kernel, seq=S, lane0=lane0),
        grid=(B,),
        in_specs=[
            pl.BlockSpec((None, S, LANES), lambda b: (b, 0, 0)),
            pl.BlockSpec((1, LANES), lambda b: (0, 0)),
            pl.BlockSpec((LANES, LANES), lambda b: (0, 0)),
        ],
        out_specs=[
            pl.BlockSpec((None, S, LANES), lambda b: (b, 0, 0)),
            pl.BlockSpec((None, FOX_HEADS, S), lambda b: (b, 0, 0)),
        ],
        out_shape=[jax.ShapeDtypeStruct((B, S, LANES), F32), jax.ShapeDtypeStruct((B, FOX_HEADS, S), F32)],
        compiler_params=_cparams(("parallel",)),
        name="forget_cumsum",
    )(small_proj.reshape(B, S, LANES), fbias, jnp.asarray(tri, BF16))


def _fox_kernel(q_ref, k_ref, v_ref, ccol_ref, crow_ref, o_ref, *, tq, seq, lane0):
    i = pl.program_id(1)
    t_col = i * tq + lax.broadcasted_iota(I32, (tq, 1), 0)
    for span in range(CAUSAL_SPANS):
        klen = (span + 1) * (seq // CAUSAL_SPANS)

        @pl.when(_causal_span((i + 1) * tq, seq) == span)
        def _(klen=klen):
            causal = lax.broadcasted_iota(I32, (tq, klen), 1) <= t_col
            for h in range(FOX_HEADS):
                hs = slice(h * HEAD_DIM, (h + 1) * HEAD_DIM)
                s = _dot_nt(q_ref[:, hs], k_ref[0:klen, hs])
                s = s + ccol_ref[:, lane0 + h:lane0 + h + 1] - crow_ref[h:h + 1, 0:klen]
                s = jnp.where(causal, s, NEG_INF)
                e = jnp.exp(s - jnp.max(s, axis=-1, keepdims=True))
                o = _dot(e.astype(BF16), v_ref[0:klen, hs]) * (1.0 / jnp.sum(e, axis=-1, keepdims=True))
                o_ref[:, hs] = o.astype(o_ref.dtype)


def _fox(plain_proj, ccol, crow, B, S, lane0, tq=256):
    pp = plain_proj.reshape(B, S, -1)
    fw = FOX_HEADS * HEAD_DIM
    return pl.pallas_call(
        functools.partial(_fox_kernel, tq=tq, seq=S, lane0=lane0),
        grid=(B, S // tq),
        in_specs=[
            pl.BlockSpec((None, tq, fw), lambda b, i: (b, i, 1)),
            pl.BlockSpec((None, S, fw), lambda b, i: (b, 0, 2)),
            pl.BlockSpec((None, S, fw), lambda b, i: (b, 0, 3)),
            pl.BlockSpec((None, tq, LANES), lambda b, i: (b, i, 0)),
            pl.BlockSpec((None, FOX_HEADS, S), lambda b, i: (b, 0, 0)),
        ],
        out_specs=pl.BlockSpec((None, tq, fw), lambda b, i: (b, i, 0)),
        out_shape=jax.ShapeDtypeStruct((B, S, fw), BF16),
        compiler_params=_cparams(("parallel", "parallel")),
        name="fox",
    )(pp, pp, pp, ccol, crow)


def _merge_kernel(on_ref, of_ref, wn_ref, wf_ref, ga_ref, gb_ref, o_ref):
    a = _dot(on_ref[...], wn_ref[...])
    b = _dot(of_ref[...], wf_ref[...])
    o_ref[...] = (ga_ref[...].astype(F32) * a + gb_ref[...].astype(F32) * b).astype(o_ref.dtype)


def _merge(o_nsa, o_fox, w_up_nsa, w_up_fox, gates, d_model, tm=1024, tn=512):
    T, K = o_nsa.shape
    nb = d_model // tn
    return pl.pallas_call(
        _merge_kernel,
        grid=(T // tm, nb),
        in_specs=[
            pl.BlockSpec((tm, K), lambda i, j: (i, 0)),
            pl.BlockSpec((tm, K), lambda i, j: (i, 0)),
            pl.BlockSpec((K, tn), lambda i, j: (0, j)),
            pl.BlockSpec((K, tn), lambda i, j: (0, j)),
            pl.BlockSpec((tm, tn), lambda i, j: (i, j)),
            pl.BlockSpec((tm, tn), lambda i, j: (i, j + nb)),
        ],
        out_specs=pl.BlockSpec((tm, tn), lambda i, j: (i, j)),
        out_shape=jax.ShapeDtypeStruct((T, d_model), BF16),
        compiler_params=_cparams(("parallel", "parallel")),
        name="merge",
    )(o_nsa, o_fox, w_up_nsa, w_up_fox, gates, gates)


def _rope_tables(pos):
    inv_freq = ROPE_THETA ** (-jnp.arange(0, HEAD_DIM, 2, dtype=F32) / HEAD_DIM)
    ang = pos.astype(F32)[:, None] * inv_freq[None, :]
    c, s = jnp.cos(ang), jnp.sin(ang)
    return jnp.concatenate([c, c], axis=-1), jnp.concatenate([-s, s], axis=-1)


def _mixer(x2, B, S, norm_gain, w_in, nsa_gate_bias, fox_forget_bias, merge_gate_bias, k_cmp_pos, k_cmp_w1, k_cmp_w2,
           v_cmp_pos, v_cmp_w1, v_cmp_w2, w_up_nsa, w_up_fox, w_out):
    T, D = x2.shape
    qw = NSA_HEADS * HEAD_DIM
    gw = NSA_GROUPS * HEAD_DIM
    fw = FOX_HEADS * HEAD_DIM
    ngate = 3 * NSA_HEADS
    widths = (qw, gw, gw, gw, gw, gw, gw, ngate, fw, fw, fw, FOX_HEADS, D, D)
    offs = np.concatenate([[0], np.cumsum(widths)])
    (w_qn, w_kc, w_vc, w_ks, w_vs, w_kw, w_vw, w_gn, w_qf, w_kf, w_vf, w_ff, w_ga, w_gb) = [
        w_in[:, offs[j]:offs[j + 1]] for j in range(len(widths))]
    w_rope = jnp.concatenate([w_qn, w_ks, w_kw], axis=1).astype(BF16)
    w_plain = jnp.concatenate([w_kc, w_vc, w_vs, w_vw, w_qf, w_kf, w_vf], axis=1).astype(BF16)
    w_gate = jnp.concatenate([w_ga, w_gb], axis=1).astype(BF16)
    pad = LANES - ngate - FOX_HEADS
    w_small = jnp.concatenate([w_gn, w_ff, jnp.zeros((D, pad), w_in.dtype)], axis=1).astype(BF16)
    gate_bias = jnp.concatenate([nsa_gate_bias, jnp.zeros((LANES - ngate,), F32)]).reshape(1, LANES)
    fbias = jnp.concatenate([jnp.zeros((ngate,), F32), fox_forget_bias, jnp.zeros((pad,), F32)]).reshape(1, LANES)

    (h,) = _rmsnorm(x2, norm_gain, (BF16,))
    cos_t, sin_t = _rope_tables(jnp.arange(S))
    rope_proj = _matmul(h, w_rope, BF16, "rope", (cos_t, sin_t), seq=S, scaled_cols=(0, qw), name="inproj_rope")
    plain_proj = _matmul(h, w_plain, BF16, scaled_cols=(4 * gw, 4 * gw + fw), name="inproj_plain")
    gates = _matmul(h, w_gate, BF16, "sigmoid_bias", (merge_gate_bias.reshape(1, 2 * D),), name="inproj_gate")
    small_proj = _matmul(h, w_small, F32, name="inproj_small")

    n_chunks = S // CMP_STRIDE
    ck = CMP_STRIDE * HEAD_DIM

    def chunks(col0):
        c = plain_proj[:, col0:col0 + gw].reshape(B, S, NSA_GROUPS, HEAD_DIM).transpose(0, 2, 1, 3)
        return c.reshape(B * NSA_GROUPS, n_chunks, ck)

    cos_c, sin_c = _rope_tables(jnp.arange(n_chunks) * CMP_STRIDE + CMP_BLOCK - 1)
    kc = _compress(chunks(0), k_cmp_pos, k_cmp_w1, k_cmp_w2, cos_c, sin_c, True)
    vc = _compress(chunks(gw), v_cmp_pos, v_cmp_w1, v_cmp_w2, cos_c, sin_c, False)
    kc = kc.reshape(B, NSA_GROUPS, n_chunks, HEAD_DIM)
    vc = vc.reshape(B, NSA_GROUPS, n_chunks, HEAD_DIM)

    o_nsa = _nsa(rope_proj, plain_proj, kc, vc, small_proj, gate_bias, B, S)
    ccol, crow = _forget_cumsum(small_proj, fbias, B, S, ngate)
    o_fox = _fox(plain_proj, ccol, crow, B, S, ngate)

    merged = _merge(o_nsa.reshape(T, qw), o_fox.reshape(T, fw), w_up_nsa.astype(BF16), w_up_fox.astype(BF16), gates, D)
    return _matmul(merged, w_out.astype(BF16), F32, "residual", (x2,), name="outproj")


def _route_kernel(xn_ref, wq_ref, k1_ref, k2_ref, e_ref, g_ref, *, tm):
    q = _dot(xn_ref[...], wq_ref[...])
    s1 = _dot_nt(k1_ref[...], q[:, :PEER_HALF].astype(BF16))
    s2 = _dot_nt(k2_ref[...], q[:, PEER_HALF:].astype(BF16))
    nc = PEER_TOPK * PEER_TOPK
    key_row = lax.broadcasted_iota(I32, (PEER_N_KEYS, tm), 0).astype(F32)
    rank_row = lax.broadcasted_iota(I32, (PEER_TOPK, tm), 0)
    cand_row = lax.broadcasted_iota(I32, (nc, tm), 0).astype(F32)

    def top_keys(s):
        vals = jnp.zeros((PEER_TOPK, tm), F32)
        idxs = jnp.zeros((PEER_TOPK, tm), F32)
        for r in range(PEER_TOPK):
            m = jnp.max(s, axis=0, keepdims=True)
            idx = jnp.min(jnp.where(s == m, key_row, float(PEER_N_KEYS)), axis=0, keepdims=True)
            vals = jnp.where(rank_row == r, m, vals)
            idxs = jnp.where(rank_row == r, idx, idxs)
            s = jnp.where(key_row == idx, -jnp.inf, s)
        return vals, idxs

    v1, i1 = top_keys(s1)
    v2, i2 = top_keys(s2)
    cand = jnp.concatenate([v1[k:k + 1, :] + v2 for k in range(PEER_TOPK)], axis=0)
    cidx = jnp.concatenate([i1[k:k + 1, :] * float(PEER_N_KEYS) + i2 for k in range(PEER_TOPK)], axis=0)

    top_s = jnp.zeros((PEER_TOPK, tm), F32)
    top_e = jnp.zeros((PEER_TOPK, tm), F32)
    for r in range(PEER_TOPK):
        m = jnp.max(cand, axis=0, keepdims=True)
        pos = jnp.min(jnp.where(cand == m, cand_row, float(nc)), axis=0, keepdims=True)
        hit = cand_row == pos
        e = jnp.sum(jnp.where(hit, cidx, 0.0), axis=0, keepdims=True)
        top_s = jnp.where(rank_row == r, m, top_s)
        top_e = jnp.where(rank_row == r, e, top_e)
        cand = jnp.where(hit, -jnp.inf, cand)

    ex = jnp.exp(top_s - top_s[0:1, :])
    e_ref[...] = top_e.astype(I32)
    g_ref[...] = ex / jnp.sum(ex, axis=0, keepdims=True)


def _route(xn_bf16, w_query, sub_keys_1, sub_keys_2, tm=256):
    T, D = xn_bf16.shape
    kd = 2 * PEER_HALF
    return pl.pallas_call(
        functools.partial(_route_kernel, tm=tm),
        grid=(T // tm, PEER_HEADS),
        in_specs=[
            pl.BlockSpec((tm, D), lambda i, h: (i, 0)),
            pl.BlockSpec((D, kd), lambda i, h: (0, h)),
            pl.BlockSpec((None, PEER_N_KEYS, PEER_HALF), lambda i, h: (h, 0, 0)),
            pl.BlockSpec((None, PEER_N_KEYS, PEER_HALF), lambda i, h: (h, 0, 0)),
        ],
        out_specs=[pl.BlockSpec((PEER_TOPK, tm), lambda i, h: (h, i))] * 2,
        out_shape=[jax.ShapeDtypeStruct((PEER_SLOTS, T), I32), jax.ShapeDtypeStruct((PEER_SLOTS, T), F32)],
        compiler_params=_cparams(("parallel", "parallel")),
        name="peer_route",
    )(xn_bf16, w_query.astype(BF16), sub_keys_1.astype(BF16), sub_keys_2.astype(BF16))


def _slot_sort_kernel(e_ref, g_ref, se_ref, sg_ref, meta_ref, *, tm, part_shift):
    row = lax.broadcasted_iota(I32, (PEER_SLOTS, tm), 0)
    e = e_ref[...]
    g = g_ref[...]
    key = e * PEER_SLOTS + row
    rank = jnp.zeros((PEER_SLOTS, tm), I32)
    for k in range(PEER_SLOTS):
        rank = rank + (key[k:k + 1, :] < key).astype(I32)
    se = jnp.zeros((PEER_SLOTS, tm), I32)
    sg = jnp.zeros((PEER_SLOTS, tm), F32)
    for k in range(PEER_SLOTS):
        hit = rank[k:k + 1, :] == row
        se = jnp.where(hit, e[k:k + 1, :], se)
        sg = jnp.where(hit, g[k:k + 1, :], sg)
    part = se >> part_shift
    meta = jnp.zeros((PEER_SLOTS, tm), I32)
    start = jnp.zeros((1, tm), I32)
    for p in range(PEER_PARTS):
        cnt = jnp.sum((part == p).astype(I32), axis=0, keepdims=True)
        meta = jnp.where(row == p, start, meta)
        meta = jnp.where(row == PEER_PARTS + p, cnt, meta)
        start = start + cnt
    se_ref[...] = (se & ((1 << part_shift) - 1)).T
    sg_ref[...] = sg.T
    meta_ref[...] = meta.T


def _slot_sort(e, g, part_shift, tm=128):
    T = e.shape[1]
    in_spec = pl.BlockSpec((PEER_SLOTS, tm), lambda i: (0, i))
    out_spec = pl.BlockSpec((tm, PEER_SLOTS), lambda i: (i, 0))
    return pl.pallas_call(
        functools.partial(_slot_sort_kernel, tm=tm, part_shift=part_shift),
        grid=(T // tm,),
        in_specs=[in_spec, in_spec],
        out_specs=[out_spec, out_spec, out_spec],
        out_shape=[jax.ShapeDtypeStruct((T, PEER_SLOTS), I32), jax.ShapeDtypeStruct((T, PEER_SLOTS), F32),
                   jax.ShapeDtypeStruct((T, PEER_SLOTS), I32)],
        compiler_params=_cparams(("parallel",)),
        name="peer_slot_sort",
    )(e, g)


SUBLANES = 8
META_W = 2 * PEER_PARTS
_BITREV3 = (0, 4, 2, 6, 1, 5, 3, 7)


GROUP_TILES = 2
GROUP_SLOTS = GROUP_TILES * SUBLANES
_GROUP_SHIFT = GROUP_SLOTS.bit_length() - 1
_LAST_GROUP = PEER_SLOTS // GROUP_SLOTS - 1


def _slot_groups(meta_ref, t, p):
    start = meta_ref[t * META_W + p]
    cnt = meta_ref[t * META_W + PEER_PARTS + p]
    g0 = lax.shift_right_logical(start, _GROUP_SHIFT)
    g1 = jnp.where(cnt > 0, lax.shift_right_logical(start + cnt + (GROUP_SLOTS - 1), _GROUP_SHIFT), g0)
    return start, cnt, g0, g1


def _gather_rows(tab_ref, se_ref, t, g, order):
    base = t * PEER_SLOTS + g * GROUP_SLOTS
    return tuple(tab_ref[se_ref[base + tile * SUBLANES + i]] for tile in range(GROUP_TILES) for i in order)


def _fold_rows(a, b, first, sh):
    return jnp.where(first, a + pltpu.roll(a, SUBLANES - sh, axis=0), b + pltpu.roll(b, sh, axis=0))


def _peer_down_kernel(u_ref, x_ref, se_ref, meta_ref, sg_ref, w_ref, r_scr, a_scr, *, tb):
    p = pl.program_id(0)
    sub = lax.broadcasted_iota(I32, (SUBLANES, LANES), 0)
    m4, m2, m1 = sub < 4, (sub & 2) == 0, (sub & 1) == 0
    lane1 = lax.broadcasted_iota(I32, (1, LANES), 1)

    @pl.when((p == 0) & (pl.program_id(1) == 0))
    def _():
        r_scr[...] = jnp.zeros_like(r_scr)

    def token(t, carry):
        _, _, g0, g1 = _slot_groups(meta_ref, t, p)
        xt = x_ref[t]

        def fold_group(g, rows):
            base = pl.multiple_of(t * PEER_SLOTS + g * GROUP_SLOTS, GROUP_SLOTS)
            for tile in range(GROUP_TILES):
                parts = []
                for k in range(SUBLANES):
                    pr = xt * rows[tile * SUBLANES + k].astype(F32)
                    parts.append(pr[0:8] + pr[8:16])
                q = [_fold_rows(parts[2 * m], parts[2 * m + 1], m4, 4) for m in range(4)]
                h = [_fold_rows(q[0], q[1], m2, 2), _fold_rows(q[2], q[3], m2, 2)]
                r_scr[pl.ds(base + tile * SUBLANES, SUBLANES), :] = _fold_rows(h[0], h[1], m1, 1)

        def group(g, rows):
            nxt = _gather_rows(u_ref, se_ref, t, jnp.minimum(g + 1, _LAST_GROUP), _BITREV3)
            fold_group(g, rows)
            return nxt

        lax.fori_loop(g0, g1, group, _gather_rows(u_ref, se_ref, t, jnp.minimum(g0, _LAST_GROUP), _BITREV3))
        return carry

    lax.fori_loop(0, tb, token, 0)

    def finish(b, carry):
        for tl in range(SUBLANES):
            t = b * SUBLANES + tl
            start = meta_ref[t * META_W + p]
            cnt = meta_ref[t * META_W + PEER_PARTS + p]
            r_t = r_scr[pl.ds(pl.multiple_of(t * PEER_SLOTS, PEER_SLOTS), PEER_SLOTS), :]
            a_row = jnp.sum(r_t.T, axis=0, keepdims=True)
            valid = (lane1 >= start) & (lane1 < start + cnt)
            a_scr[pl.ds(t, 1), :] = jnp.where(valid, a_row, 0.0)
        return carry

    lax.fori_loop(0, tb // SUBLANES, finish, 0)
    w_ref[...] = jax.nn.gelu(a_scr[...]) * sg_ref[...]


def _peer_up_kernel(v_ref, y_ref, se_ref, meta_ref, w_ref, o_ref, *, tb):
    p = pl.program_id(0)
    order = tuple(range(SUBLANES))

    def token(t, carry):
        _, _, g0, g1 = _slot_groups(meta_ref, t, p)

        def group(g, acc):
            rows = _gather_rows(v_ref, se_ref, t, g, order)
            base = t * PEER_SLOTS + g * GROUP_SLOTS
            for i in range(GROUP_SLOTS):
                acc = acc + w_ref[base + i] * rows[i].astype(F32)
            return acc

        o_ref[t] = lax.fori_loop(g0, g1, group, y_ref[t])
        return carry

    lax.fori_loop(0, tb, token, 0)


def _peer_experts(xn3, resid3, se, sg, meta, down_tab, up_tab, tb=64):
    T = xn3.shape[0]
    E = down_tab.shape[0]
    part_size = E // PEER_PARTS
    nt = T // tb
    rows = xn3.shape[1]
    se_flat = se.reshape(T * PEER_SLOTS)
    meta_flat = meta[:, :META_W].reshape(T * META_W)
    tab_spec = pl.BlockSpec((part_size, rows, LANES), lambda p, i: (p, 0, 0), pipeline_mode=pl.Buffered(1))
    tok_spec = pl.BlockSpec((tb, rows, LANES), lambda p, i: (i, 0, 0))
    se_spec = pl.BlockSpec((tb * PEER_SLOTS,), lambda p, i: (i,), memory_space=pltpu.SMEM)
    meta_spec = pl.BlockSpec((tb * META_W,), lambda p, i: (i,), memory_space=pltpu.SMEM)
    w_parts = pl.pallas_call(
        functools.partial(_peer_down_kernel, tb=tb),
        grid=(PEER_PARTS, nt),
        in_specs=[tab_spec, tok_spec, se_spec, meta_spec, pl.BlockSpec((tb, PEER_SLOTS), lambda p, i: (i, 0))],
        out_specs=pl.BlockSpec((None, tb, PEER_SLOTS), lambda p, i: (p, i, 0)),
        out_shape=jax.ShapeDtypeStruct((PEER_PARTS, T, PEER_SLOTS), F32),
        scratch_shapes=[pltpu.VMEM((tb * PEER_SLOTS, LANES), F32), pltpu.VMEM((tb, PEER_SLOTS), F32)],
        compiler_params=_cparams(("arbitrary", "arbitrary")),
        name="peer_down",
    )(down_tab, xn3, se_flat, meta_flat, sg)
    w_flat = w_parts.reshape(PEER_PARTS * T * PEER_SLOTS)
    w_spec = pl.BlockSpec((tb * PEER_SLOTS,), lambda p, i: (p * nt + i,), memory_space=pltpu.SMEM)
    return pl.pallas_call(
        functools.partial(_peer_up_kernel, tb=tb),
        grid=(PEER_PARTS, nt),
        in_specs=[tab_spec, tok_spec, se_spec, meta_spec, w_spec],
        out_specs=tok_spec,
        out_shape=jax.ShapeDtypeStruct(resid3.shape, F32),
        input_output_aliases={1: 0},
        compiler_params=_cparams(("arbitrary", "arbitrary")),
        name="peer_up",
    )(up_tab, resid3, se_flat, meta_flat, w_flat)


def _peer(x2, norm_gain, w_query, sub_keys_1, sub_keys_2, expert_down, expert_up):
    T, D = x2.shape
    E = expert_down.shape[0]
    rows = D // LANES
    xn_bf16, xn = _rmsnorm(x2, norm_gain, (BF16, F32))
    e, g = _route(xn_bf16, w_query, sub_keys_1, sub_keys_2)
    part_shift = int(math.log2(E // PEER_PARTS))
    se, sg, meta = _slot_sort(e, g, part_shift)
    down_tab = expert_down.astype(BF16).reshape(E, rows, LANES)
    up_tab = expert_up.astype(BF16).reshape(E, rows, LANES)
    y = _peer_experts(xn.reshape(T, rows, LANES), x2.reshape(T, rows, LANES), se, sg, meta, down_tab, up_tab)
    return y.reshape(T, D)


def kernel(x, norm_mix_gain, w_in, nsa_gate_bias, fox_forget_bias, merge_gate_bias, k_cmp_pos, k_cmp_w1, k_cmp_w2, v_cmp_pos, v_cmp_w1, v_cmp_w2, w_up_nsa, w_up_fox, w_out, norm_ffn_gain, peer_w_query, peer_sub_keys_1, peer_sub_keys_2, peer_expert_down, peer_expert_up, norm_final_gain):
    B, S, D = x.shape
    x2 = x.reshape(B * S, D)
    for l in range(norm_mix_gain.shape[0]):
        x2 = _mixer(x2, B, S, norm_mix_gain[l], w_in[l], nsa_gate_bias[l], fox_forget_bias[l], merge_gate_bias[l],
                    k_cmp_pos[l], k_cmp_w1[l], k_cmp_w2[l], v_cmp_pos[l], v_cmp_w1[l], v_cmp_w2[l],
                    w_up_nsa[l], w_up_fox[l], w_out[l])
        x2 = _peer(x2, norm_ffn_gain[l], peer_w_query[l], peer_sub_keys_1[l], peer_sub_keys_2[l], peer_expert_down[l],
                   peer_expert_up[l])
    (out,) = _rmsnorm(x2, norm_final_gain, (F32,))
    return out.reshape(B, S, D)
```

```python
import functools
import math

import jax
import jax.numpy as jnp
import numpy as np
from jax import lax
from jax.experimental import pallas as pl
from jax.experimental.pallas import tpu as pltpu

F32 = jnp.float32
BF16 = jnp.bfloat16
I32 = jnp.int32

LANES = 128
HEAD_DIM = 128
ROPE_THETA = 10000.0
NORM_EPS = 1e-6
NEG_INF = -1e30
TINY = 1e-30
PAD_SCORE = -3e38

NSA_HEADS = 8
NSA_GROUPS = 2
NSA_HPG = NSA_HEADS // NSA_GROUPS
CMP_BLOCK = 32
CMP_STRIDE = 16
SEL_BLOCK = 64
SEL_TOPK = 16
FORCED_SCORE = 1e4
WINDOW = 512
FOX_HEADS = 8

PEER_HEADS = 8
PEER_N_KEYS = 128
PEER_HALF = 128
PEER_TOPK = 16
PEER_SLOTS = PEER_HEADS * PEER_TOPK
PEER_PARTS = 2

VMEM_LIMIT_BYTES = 56 * 1024 * 1024


def _cparams(sem):
    return pltpu.CompilerParams(dimension_semantics=sem, vmem_limit_bytes=VMEM_LIMIT_BYTES)


def _split3(x):
    hi = x.astype(BF16)
    r = x - hi.astype(F32)
    mid = r.astype(BF16)
    lo = (r - mid.astype(F32)).astype(BF16)
    return hi, mid, lo


def _dot(a, b):
    return jnp.dot(a, b, preferred_element_type=F32)


def _dot_nt(a, b):
    return lax.dot_general(a, b, (((1,), (1,)), ((), ())), preferred_element_type=F32)


def _dot3_right(x, m01):
    hi, mid, lo = _split3(x)
    return _dot(hi, m01) + _dot(mid, m01) + _dot(lo, m01)


def _dot3_left(m01, x):
    hi, mid, lo = _split3(x)
    return _dot(m01, hi) + _dot(m01, mid) + _dot(m01, lo)


def _masked_softmax(s, mask):
    s = jnp.where(mask, s, NEG_INF)
    m = jnp.max(s, axis=-1, keepdims=True)
    e = jnp.where(mask, jnp.exp(s - m), 0.0)
    return e / jnp.maximum(jnp.sum(e, axis=-1, keepdims=True), TINY)


def _rope_tile(x, cos_full, sin_signed):
    return x * cos_full + pltpu.roll(x, HEAD_DIM // 2, axis=1) * sin_signed


def _rmsnorm_kernel(x_ref, g_ref, *o_refs):
    x = x_ref[...]
    y = x * lax.rsqrt(jnp.mean(x * x, axis=-1, keepdims=True) + NORM_EPS) * g_ref[...]
    for o_ref in o_refs:
        o_ref[...] = y.astype(o_ref.dtype)


def _rmsnorm(x2, gain, out_dtypes, tm=512):
    T, D = x2.shape
    outs = pl.pallas_call(
        _rmsnorm_kernel,
        grid=(T // tm,),
        in_specs=[pl.BlockSpec((tm, D), lambda i: (i, 0)), pl.BlockSpec((1, D), lambda i: (0, 0))],
        out_specs=[pl.BlockSpec((tm, D), lambda i: (i, 0)) for _ in out_dtypes],
        out_shape=[jax.ShapeDtypeStruct((T, D), dt) for dt in out_dtypes],
        compiler_params=_cparams(("parallel",)),
        name="rmsnorm",
    )(x2, gain.reshape(1, D).astype(F32))
    return outs


def _mm_kernel(a_ref, b_ref, *rest, epilogue, scaled_blocks):
    o_ref = rest[-1]
    acc = _dot(a_ref[...], b_ref[...])
    if scaled_blocks is not None:
        j = pl.program_id(1)
        acc = acc * jnp.where((j >= scaled_blocks[0]) & (j < scaled_blocks[1]), HEAD_DIM ** -0.5, 1.0)
    if epilogue == "rope":
        cos_ref, sin_ref = rest[0], rest[1]
        c, s = cos_ref[...], sin_ref[...]
        for j in range(acc.shape[1] // HEAD_DIM):
            sl = slice(j * HEAD_DIM, (j + 1) * HEAD_DIM)
            o_ref[:, sl] = _rope_tile(acc[:, sl], c, s).astype(o_ref.dtype)
    elif epilogue == "sigmoid_bias":
        o_ref[...] = jax.nn.sigmoid(acc + rest[0][...]).astype(o_ref.dtype)
    elif epilogue == "residual":
        o_ref[...] = (rest[0][...] + acc).astype(o_ref.dtype)
    else:
        o_ref[...] = acc.astype(o_ref.dtype)


def _matmul(a, b, out_dtype, epilogue="none", extras=(), tm=1024, tn=512, seq=None, scaled_cols=None, name="matmul"):
    M, K = a.shape
    _, N = b.shape
    tn = min(tn, N)
    tm = min(tm, M)
    scaled_blocks = None if scaled_cols is None else (scaled_cols[0] // tn, scaled_cols[1] // tn)
    in_specs = [pl.BlockSpec((tm, K), lambda i, j: (i, 0)), pl.BlockSpec((K, tn), lambda i, j: (0, j))]
    if epilogue == "rope":
        nrep = seq // tm
        in_specs += [pl.BlockSpec((tm, HEAD_DIM), lambda i, j: (i % nrep, 0))] * 2
    elif epilogue == "sigmoid_bias":
        in_specs += [pl.BlockSpec((1, tn), lambda i, j: (0, j))]
    elif epilogue == "residual":
        in_specs += [pl.BlockSpec((tm, tn), lambda i, j: (i, j))]
    return pl.pallas_call(
        functools.partial(_mm_kernel, epilogue=epilogue, scaled_blocks=scaled_blocks),
        grid=(M // tm, N // tn),
        in_specs=in_specs,
        out_specs=pl.BlockSpec((tm, tn), lambda i, j: (i, j)),
        out_shape=jax.ShapeDtypeStruct((M, N), out_dtype),
        compiler_params=_cparams(("parallel", "parallel")),
        name=name,
    )(a, b, *extras)


def _compress_kernel(ch_ref, pos_ref, w1t_ref, w1b_ref, w2_ref, cos_ref, sin_ref, o_ref, *, rope):
    ch = ch_ref[...].astype(F32)
    a_top = (ch + pos_ref[0:1, :]).astype(BF16)
    a_bot = (ch + pos_ref[1:2, :]).astype(BF16)
    y_top = _dot(a_top, w1t_ref[...])
    y_bot = _dot(a_bot, w1b_ref[...])
    n = y_bot.shape[0]
    hidden = y_top + pltpu.roll(y_bot, n - 1, axis=0)
    out = _dot(jax.nn.gelu(hidden).astype(BF16), w2_ref[...])
    if rope:
        out = _rope_tile(out, cos_ref[...], sin_ref[...])
    row = lax.broadcasted_iota(I32, out.shape, 0)
    o_ref[...] = jnp.where(row < n - 1, out, 0.0).astype(o_ref.dtype)


def _compress(chunks, pos, w1, w2, cos_c, sin_c, rope):
    BG, NC, CK = chunks.shape
    hid = w1.shape[1]
    pos2 = pos.reshape(2, CK).astype(F32)
    return pl.pallas_call(
        functools.partial(_compress_kernel, rope=rope),
        grid=(BG,),
        in_specs=[
            pl.BlockSpec((None, NC, CK), lambda i: (i, 0, 0)),
            pl.BlockSpec((2, CK), lambda i: (0, 0)),
            pl.BlockSpec((CK, hid), lambda i: (0, 0)),
            pl.BlockSpec((CK, hid), lambda i: (1, 0)),
            pl.BlockSpec((hid, HEAD_DIM), lambda i: (0, 0)),
            pl.BlockSpec((NC, HEAD_DIM), lambda i: (0, 0)),
            pl.BlockSpec((NC, HEAD_DIM), lambda i: (0, 0)),
        ],
        out_specs=pl.BlockSpec((None, NC, HEAD_DIM), lambda i: (i, 0, 0)),
        out_shape=jax.ShapeDtypeStruct((BG, NC, HEAD_DIM), BF16),
        compiler_params=_cparams(("parallel",)),
        name="compress",
    )(chunks, pos2, w1.astype(BF16), w1.astype(BF16), w2.astype(BF16), cos_c, sin_c)


CAUSAL_SPANS = 4


def _causal_span(q_end, seq):
    return lax.shift_right_logical(q_end - 1, (seq // CAUSAL_SPANS).bit_length() - 1)


def _attend(q, k, v, mask):
    s = jnp.where(mask, _dot_nt(q, k), NEG_INF)
    e = jnp.exp(s - jnp.max(s, axis=-1, keepdims=True))
    return _dot(e.astype(BF16), v) * (1.0 / jnp.sum(e, axis=-1, keepdims=True))


def _nsa_kernel(q_ref, ks_ref, kw_ref, vs_ref, vw_ref, kc_ref, vc_ref, gate_ref, gbias_ref, ovl_ref, exp_ref,
                o_ref, osel_scr, *, tq, seq):
    g = pl.program_id(1)
    t0 = pl.program_id(2) * tq
    n_sel = seq // SEL_BLOCK
    n_cmp = seq // CMP_STRIDE - CMP_BLOCK // CMP_STRIDE + 1
    wlen = tq + WINDOW

    t_col = t0 + lax.broadcasted_iota(I32, (tq, 1), 0)
    lane = lax.broadcasted_iota(I32, (tq, LANES), 1)
    gate_all = jax.nn.sigmoid(gate_ref[...] + gbias_ref[...])

    def gate(h, j):
        cols = [gate_all[:, (gg * NSA_HPG + h) * 3 + j:(gg * NSA_HPG + h) * 3 + j + 1] for gg in range(NSA_GROUPS)]
        out = cols[0]
        for gg in range(1, NSA_GROUPS):
            out = jnp.where(g == gg, cols[gg], out)
        return out

    cmp_mask = (lane * CMP_STRIDE + (CMP_BLOCK - 1) <= t_col) & (lane < n_cmp)
    wstart = pl.multiple_of(jnp.maximum(t0 - WINDOW, 0), tq)
    key_w = wstart + lax.broadcasted_iota(I32, (tq, wlen), 1)
    win_mask = (key_w <= t_col) & (t_col - key_w < WINDOW)

    cur = t_col >> 6
    forced = (lane == 0) | (lane == cur) | (lane == cur - 1)
    future = lane * SEL_BLOCK > t_col

    heads = [slice(h * HEAD_DIM, (h + 1) * HEAD_DIM) for h in range(NSA_HPG)]
    kc = kc_ref[...]
    vc = vc_ref[...]
    o_cmp = []
    psum = jnp.zeros((tq, LANES), F32)
    for h in range(NSA_HPG):
        p = _masked_softmax(_dot_nt(q_ref[:, heads[h]], kc), cmp_mask)
        psum = psum + p
        o_cmp.append(_dot(p.astype(BF16), vc))
    imp = _dot3_right(psum, ovl_ref[...])
    score = jnp.where(forced, FORCED_SCORE, imp)
    score = jnp.where(future, NEG_INF, score)
    score = jnp.where(lane < n_sel, score, PAD_SCORE)
    rank = jnp.zeros((tq, LANES), I32)
    for k in range(n_sel):
        ck = score[:, k:k + 1]
        beats = (ck > score) | ((ck == score) & (lane > k))
        rank = rank + beats.astype(I32)
    sel = ((rank < min(SEL_TOPK, n_sel)) & (lane < n_sel)).astype(BF16)

    for span in range(CAUSAL_SPANS):
        klen = (span + 1) * (seq // CAUSAL_SPANS)

        @pl.when(_causal_span(t0 + tq, seq) == span)
        def _(klen=klen):
            causal = lax.broadcasted_iota(I32, (tq, klen), 1) <= t_col
            sel_mask = (_dot(sel, exp_ref[:, 0:klen]) > 0.5) & causal
            for h in range(NSA_HPG):
                osel_scr[:, heads[h]] = gate(h, 1) * _attend(q_ref[:, heads[h]], ks_ref[0:klen, :], vs_ref[0:klen, :], sel_mask)

    kw = kw_ref[pl.ds(wstart, wlen), :]
    vw = vw_ref[pl.ds(wstart, wlen), :]
    for h in range(NSA_HPG):
        o_win = _attend(q_ref[:, heads[h]], kw, vw, win_mask)
        o = gate(h, 0) * o_cmp[h] + osel_scr[:, heads[h]] + gate(h, 2) * o_win
        o_ref[:, heads[h]] = o.astype(o_ref.dtype)


def _nsa(rope_proj, plain_proj, kc, vc, small_proj, gate_bias, B, S, tq=256):
    n_sel = S // SEL_BLOCK
    n_cmp = S // CMP_STRIDE - CMP_BLOCK // CMP_STRIDE + 1
    c_start = np.arange(LANES) * CMP_STRIDE
    s_start = np.arange(LANES) * SEL_BLOCK
    ovl = (c_start[:, None] < s_start[None, :] + SEL_BLOCK) & (s_start[None, :] < c_start[:, None] + CMP_BLOCK)
    ovl &= (np.arange(LANES)[:, None] < n_cmp) & (np.arange(LANES)[None, :] < n_sel)
    expand = (np.arange(LANES)[:, None] == (np.arange(S)[None, :] // SEL_BLOCK))
    rp = rope_proj.reshape(B, S, -1)
    pp = plain_proj.reshape(B, S, -1)
    sm = small_proj.reshape(B, S, LANES)
    qw = NSA_HEADS * HEAD_DIM
    gqw = NSA_HPG * HEAD_DIM
    nq = qw // HEAD_DIM
    kv = lambda col0: pl.BlockSpec((None, S, HEAD_DIM), lambda b, g, i: (b, 0, col0 + g))
    cmp_spec = pl.BlockSpec((None, None, LANES, HEAD_DIM), lambda b, g, i: (b, g, 0, 0))
    return pl.pallas_call(
        functools.partial(_nsa_kernel, tq=tq, seq=S),
        grid=(B, NSA_GROUPS, S // tq),
        in_specs=[
            pl.BlockSpec((None, tq, gqw), lambda b, g, i: (b, i, g)),
            kv(nq), kv(nq + NSA_GROUPS), kv(2 * NSA_GROUPS), kv(3 * NSA_GROUPS),
            cmp_spec, cmp_spec,
            pl.BlockSpec((None, tq, LANES), lambda b, g, i: (b, i, 0)),
            pl.BlockSpec((1, LANES), lambda b, g, i: (0, 0)),
            pl.BlockSpec((LANES, LANES), lambda b, g, i: (0, 0)),
            pl.BlockSpec((LANES, S), lambda b, g, i: (0, 0)),
        ],
        out_specs=pl.BlockSpec((None, tq, gqw), lambda b, g, i: (b, i, g)),
        out_shape=jax.ShapeDtypeStruct((B, S, qw), BF16),
        scratch_shapes=[pltpu.VMEM((tq, gqw), F32)],
        compiler_params=_cparams(("parallel", "parallel", "parallel")),
        name="nsa",
    )(rp, rp, rp, pp, pp, kc, vc, sm, gate_bias, jnp.asarray(ovl, BF16), jnp.asarray(expand, BF16))


def _forget_cumsum_kernel(sm_ref, fbias_ref, tri_ref, ccol_ref, crow_ref, *, seq, lane0):
    x = sm_ref[...] + fbias_ref[...]
    lane = lax.broadcasted_iota(I32, x.shape, 1)
    log_f = jnp.minimum(x, 0.0) - jnp.log1p(jnp.exp(-jnp.abs(x)))
    log_f = jnp.where((lane >= lane0) & (lane < lane0 + FOX_HEADS), log_f, 0.0)
    carry = jnp.zeros((1, LANES), F32)
    blk = tri_ref.shape[0]
    for r in range(seq // blk):
        c = _dot3_left(tri_ref[...], log_f[r * blk:(r + 1) * blk, :]) + carry
        ccol_ref[r * blk:(r + 1) * blk, :] = c
        carry = c[blk - 1:blk, :]
    crow_ref[...] = ccol_ref[...].T[lane0:lane0 + FOX_HEADS, :]


def _forget_cumsum(small_proj, fbias, B, S, lane0):
    tri = np.tril(np.ones((LANES, LANES), np.float32))
    return pl.pallas_call(
        functools.partial(_forget_cumsum_kernel, seq=S, lane0=lane0),
        grid=(B,),
        in_specs=[
            pl.BlockSpec((None, S, LANES), lambda b: (b, 0, 0)),
            pl.BlockSpec((1, LANES), lambda b: (0, 0)),
            pl.BlockSpec((LANES, LANES), lambda b: (0, 0)),
        ],
        out_specs=[
            pl.BlockSpec((None, S, LANES), lambda b: (b, 0, 0)),
            pl.BlockSpec((None, FOX_HEADS, S), lambda b: (b, 0, 0)),
        ],
        out_shape=[jax.ShapeDtypeStruct((B, S, LANES), F32), jax.ShapeDtypeStruct((B, FOX_HEADS, S), F32)],
        compiler_params=_cparams(("parallel",)),
        name="forget_cumsum",
    )(small_proj.reshape(B, S, LANES), fbias, jnp.asarray(tri, BF16))


def _fox_kernel(q_ref, k_ref, v_ref, ccol_ref, crow_ref, o_ref, *, tq, seq, lane0):
    i = pl.program_id(1)
    t_col = i * tq + lax.broadcasted_iota(I32, (tq, 1), 0)
    for span in range(CAUSAL_SPANS):
        klen = (span + 1) * (seq // CAUSAL_SPANS)

        @pl.when(_causal_span((i + 1) * tq, seq) == span)
        def _(klen=klen):
            causal = lax.broadcasted_iota(I32, (tq, klen), 1) <= t_col
            for h in range(FOX_HEADS):
                hs = slice(h * HEAD_DIM, (h + 1) * HEAD_DIM)
                s = _dot_nt(q_ref[:, hs], k_ref[0:klen, hs])
                s = s + ccol_ref[:, lane0 + h:lane0 + h + 1] - crow_ref[h:h + 1, 0:klen]
                s = jnp.where(causal, s, NEG_INF)
                e = jnp.exp(s - jnp.max(s, axis=-1, keepdims=True))
                o = _dot(e.astype(BF16), v_ref[0:klen, hs]) * (1.0 / jnp.sum(e, axis=-1, keepdims=True))
                o_ref[:, hs] = o.astype(o_ref.dtype)


def _fox(plain_proj, ccol, crow, B, S, lane0, tq=256):
    pp = plain_proj.reshape(B, S, -1)
    fw = FOX_HEADS * HEAD_DIM
    return pl.pallas_call(
        functools.partial(_fox_kernel, tq=tq, seq=S, lane0=lane0),
        grid=(B, S // tq),
        in_specs=[
            pl.BlockSpec((None, tq, fw), lambda b, i: (b, i, 1)),
            pl.BlockSpec((None, S, fw), lambda b, i: (b, 0, 2)),
            pl.BlockSpec((None, S, fw), lambda b, i: (b, 0, 3)),
            pl.BlockSpec((None, tq, LANES), lambda b, i: (b, i, 0)),
            pl.BlockSpec((None, FOX_HEADS, S), lambda b, i: (b, 0, 0)),
        ],
        out_specs=pl.BlockSpec((None, tq, fw), lambda b, i: (b, i, 0)),
        out_shape=jax.ShapeDtypeStruct((B, S, fw), BF16),
        compiler_params=_cparams(("parallel", "parallel")),
        name="fox",
    )(pp, pp, pp, ccol, crow)


def _merge_kernel(on_ref, of_ref, wn_ref, wf_ref, ga_ref, gb_ref, o_ref):
    a = _dot(on_ref[...], wn_ref[...])
    b = _dot(of_ref[...], wf_ref[...])
    o_ref[...] = (ga_ref[...].astype(F32) * a + gb_ref[...].astype(F32) * b).astype(o_ref.dtype)


def _merge(o_nsa, o_fox, w_up_nsa, w_up_fox, gates, d_model, tm=1024, tn=512):
    T, K = o_nsa.shape
    nb = d_model // tn
    return pl.pallas_call(
        _merge_kernel,
        grid=(T // tm, nb),
        in_specs=[
            pl.BlockSpec((tm, K), lambda i, j: (i, 0)),
            pl.BlockSpec((tm, K), lambda i, j: (i, 0)),
            pl.BlockSpec((K, tn), lambda i, j: (0, j)),
            pl.BlockSpec((K, tn), lambda i, j: (0, j)),
            pl.BlockSpec((tm, tn), lambda i, j: (i, j)),
            pl.BlockSpec((tm, tn), lambda i, j: (i, j + nb)),
        ],
        out_specs=pl.BlockSpec((tm, tn), lambda i, j: (i, j)),
        out_shape=jax.ShapeDtypeStruct((T, d_model), BF16),
        compiler_params=_cparams(("parallel", "parallel")),
        name="merge",
    )(o_nsa, o_fox, w_up_nsa, w_up_fox, gates, gates)


def _rope_tables(pos):
    inv_freq = ROPE_THETA ** (-jnp.arange(0, HEAD_DIM, 2, dtype=F32) / HEAD_DIM)
    ang = pos.astype(F32)[:, None] * inv_freq[None, :]
    c, s = jnp.cos(ang), jnp.sin(ang)
    return jnp.concatenate([c, c], axis=-1), jnp.concatenate([-s, s], axis=-1)


def _mixer(x2, B, S, norm_gain, w_in, nsa_gate_bias, fox_forget_bias, merge_gate_bias, k_cmp_pos, k_cmp_w1, k_cmp_w2,
           v_cmp_pos, v_cmp_w1, v_cmp_w2, w_up_nsa, w_up_fox, w_out):
    T, D = x2.shape
    qw = NSA_HEADS * HEAD_DIM
    gw = NSA_GROUPS * HEAD_DIM
    fw = FOX_HEADS * HEAD_DIM
    ngate = 3 * NSA_HEADS
    widths = (qw, gw, gw, gw, gw, gw, gw, ngate, fw, fw, fw, FOX_HEADS, D, D)
    offs = np.concatenate([[0], np.cumsum(widths)])
    (w_qn, w_kc, w_vc, w_ks, w_vs, w_kw, w_vw, w_gn, w_qf, w_kf, w_vf, w_ff, w_ga, w_gb) = [
        w_in[:, offs[j]:offs[j + 1]] for j in range(len(widths))]
    w_rope = jnp.concatenate([w_qn, w_ks, w_kw], axis=1).astype(BF16)
    w_plain = jnp.concatenate([w_kc, w_vc, w_vs, w_vw, w_qf, w_kf, w_vf], axis=1).astype(BF16)
    w_gate = jnp.concatenate([w_ga, w_gb], axis=1).astype(BF16)
    pad = LANES - ngate - FOX_HEADS
    w_small = jnp.concatenate([w_gn, w_ff, jnp.zeros((D, pad), w_in.dtype)], axis=1).astype(BF16)
    gate_bias = jnp.concatenate([nsa_gate_bias, jnp.zeros((LANES - ngate,), F32)]).reshape(1, LANES)
    fbias = jnp.concatenate([jnp.zeros((ngate,), F32), fox_forget_bias, jnp.zeros((pad,), F32)]).reshape(1, LANES)

    (h,) = _rmsnorm(x2, norm_gain, (BF16,))
    cos_t, sin_t = _rope_tables(jnp.arange(S))
    rope_proj = _matmul(h, w_rope, BF16, "rope", (cos_t, sin_t), seq=S, scaled_cols=(0, qw), name="inproj_rope")
    plain_proj = _matmul(h, w_plain, BF16, scaled_cols=(4 * gw, 4 * gw + fw), name="inproj_plain")
    gates = _matmul(h, w_gate, BF16, "sigmoid_bias", (merge_gate_bias.reshape(1, 2 * D),), name="inproj_gate")
    small_proj = _matmul(h, w_small, F32, name="inproj_small")

    n_chunks = S // CMP_STRIDE
    ck = CMP_STRIDE * HEAD_DIM

    def chunks(col0):
        c = plain_proj[:, col0:col0 + gw].reshape(B, S, NSA_GROUPS, HEAD_DIM).transpose(0, 2, 1, 3)
        return c.reshape(B * NSA_GROUPS, n_chunks, ck)

    cos_c, sin_c = _rope_tables(jnp.arange(n_chunks) * CMP_STRIDE + CMP_BLOCK - 1)
    kc = _compress(chunks(0), k_cmp_pos, k_cmp_w1, k_cmp_w2, cos_c, sin_c, True)
    vc = _compress(chunks(gw), v_cmp_pos, v_cmp_w1, v_cmp_w2, cos_c, sin_c, False)
    kc = kc.reshape(B, NSA_GROUPS, n_chunks, HEAD_DIM)
    vc = vc.reshape(B, NSA_GROUPS, n_chunks, HEAD_DIM)

    o_nsa = _nsa(rope_proj, plain_proj, kc, vc, small_proj, gate_bias, B, S)
    ccol, crow = _forget_cumsum(small_proj, fbias, B, S, ngate)
    o_fox = _fox(plain_proj, ccol, crow, B, S, ngate)

    merged = _merge(o_nsa.reshape(T, qw), o_fox.reshape(T, fw), w_up_nsa.astype(BF16), w_up_fox.astype(BF16), gates, D)
    return _matmul(merged, w_out.astype(BF16), F32, "residual", (x2,), name="outproj")


def _candidate_pieces():
    pieces = [((0, 1), (0, PEER_TOPK))]
    pieces += [((k, k + 1), (0, SUBLANES)) for k in range(1, SUBLANES)]
    pieces += [((SUBLANES, PEER_TOPK), (0, 1))]
    covered = {(a, b) for (a0, a1), (b0, b1) in pieces for a in range(a0, a1) for b in range(b0, b1)}
    assert all((a, b) in covered for a in range(PEER_TOPK) for b in range(PEER_TOPK) if (a + 1) * (b + 1) <= PEER_TOPK)
    return pieces


def _candidate_flat_index(tm):
    rows = [a * PEER_TOPK + b for (a0, a1), (b0, b1) in _candidate_pieces() for a in range(a0, a1) for b in range(b0, b1)]
    return jnp.asarray(np.broadcast_to(np.asarray(rows, np.float32)[:, None], (len(rows), tm)))


def _route_kernel(xn_ref, wq_ref, k1_ref, k2_ref, flat_ref, e_ref, g_ref, *, tm):
    q = _dot(xn_ref[...], wq_ref[...])
    s1 = _dot_nt(k1_ref[...], q[:, :PEER_HALF].astype(BF16))
    s2 = _dot_nt(k2_ref[...], q[:, PEER_HALF:].astype(BF16))
    key_row = lax.broadcasted_iota(I32, (PEER_N_KEYS, tm), 0).astype(F32)
    rank_row = lax.broadcasted_iota(I32, (PEER_TOPK, tm), 0)
    cand_row = flat_ref[...]
    nc = float(PEER_TOPK * PEER_TOPK)

    def top_keys(s):
        vals = jnp.zeros((PEER_TOPK, tm), F32)
        idxs = jnp.zeros((PEER_TOPK, tm), F32)
        for r in range(PEER_TOPK):
            m = jnp.max(s, axis=0, keepdims=True)
            idx = jnp.min(jnp.where(s == m, key_row, float(PEER_N_KEYS)), axis=0, keepdims=True)
            vals = jnp.where(rank_row == r, m, vals)
            idxs = jnp.where(rank_row == r, idx, idxs)
            s = jnp.where(key_row == idx, -jnp.inf, s)
        return vals, idxs

    v1, i1 = top_keys(s1)
    v2, i2 = top_keys(s2)
    pieces = _candidate_pieces()
    cand = jnp.concatenate([v1[a0:a1, :] + v2[b0:b1, :] for (a0, a1), (b0, b1) in pieces], axis=0)
    cidx = jnp.concatenate([i1[a0:a1, :] * float(PEER_N_KEYS) + i2[b0:b1, :] for (a0, a1), (b0, b1) in pieces],
                           axis=0)

    top_s = jnp.zeros((PEER_TOPK, tm), F32)
    top_e = jnp.zeros((PEER_TOPK, tm), F32)
    for r in range(PEER_TOPK):
        m = jnp.max(cand, axis=0, keepdims=True)
        pos = jnp.min(jnp.where(cand == m, cand_row, nc), axis=0, keepdims=True)
        hit = cand_row == pos
        e = jnp.sum(jnp.where(hit, cidx, 0.0), axis=0, keepdims=True)
        top_s = jnp.where(rank_row == r, m, top_s)
        top_e = jnp.where(rank_row == r, e, top_e)
        cand = jnp.where(hit, -jnp.inf, cand)

    ex = jnp.exp(top_s - top_s[0:1, :])
    e_ref[...] = top_e.astype(I32)
    g_ref[...] = ex / jnp.sum(ex, axis=0, keepdims=True)


def _route(xn_bf16, w_query, sub_keys_1, sub_keys_2, tm=256):
    T, D = xn_bf16.shape
    kd = 2 * PEER_HALF
    flat = _candidate_flat_index(tm)
    return pl.pallas_call(
        functools.partial(_route_kernel, tm=tm),
        grid=(T // tm, PEER_HEADS),
        in_specs=[
            pl.BlockSpec((tm, D), lambda i, h: (i, 0)),
            pl.BlockSpec((D, kd), lambda i, h: (0, h)),
            pl.BlockSpec((None, PEER_N_KEYS, PEER_HALF), lambda i, h: (h, 0, 0)),
            pl.BlockSpec((None, PEER_N_KEYS, PEER_HALF), lambda i, h: (h, 0, 0)),
            pl.BlockSpec(flat.shape, lambda i, h: (0, 0)),
        ],
        out_specs=[pl.BlockSpec((PEER_TOPK, tm), lambda i, h: (h, i))] * 2,
        out_shape=[jax.ShapeDtypeStruct((PEER_SLOTS, T), I32), jax.ShapeDtypeStruct((PEER_SLOTS, T), F32)],
        compiler_params=_cparams(("parallel", "parallel")),
        name="peer_route",
    )(xn_bf16, w_query.astype(BF16), sub_keys_1.astype(BF16), sub_keys_2.astype(BF16), flat)


def _slot_sort_kernel(e_ref, g_ref, se_ref, sg_ref, meta_ref, *, tm, part_shift):
    row = lax.broadcasted_iota(I32, (PEER_SLOTS, tm), 0)
    e = e_ref[...]
    g = g_ref[...]
    key = e * PEER_SLOTS + row
    rank = jnp.zeros((PEER_SLOTS, tm), I32)
    for k in range(PEER_SLOTS):
        rank = rank + (key[k:k + 1, :] < key).astype(I32)
    se = jnp.zeros((PEER_SLOTS, tm), I32)
    sg = jnp.zeros((PEER_SLOTS, tm), F32)
    for k in range(PEER_SLOTS):
        hit = rank[k:k + 1, :] == row
        se = jnp.where(hit, e[k:k + 1, :], se)
        sg = jnp.where(hit, g[k:k + 1, :], sg)
    part = se >> part_shift
    meta = jnp.zeros((PEER_SLOTS, tm), I32)
    start = jnp.zeros((1, tm), I32)
    for p in range(PEER_PARTS):
        cnt = jnp.sum((part == p).astype(I32), axis=0, keepdims=True)
        meta = jnp.where(row == p, start, meta)
        meta = jnp.where(row == PEER_PARTS + p, cnt, meta)
        start = start + cnt
    se_ref[...] = (se & ((1 << part_shift) - 1)).T
    sg_ref[...] = sg.T
    meta_ref[...] = meta.T


def _slot_sort(e, g, part_shift, tm=128):
    T = e.shape[1]
    in_spec = pl.BlockSpec((PEER_SLOTS, tm), lambda i: (0, i))
    out_spec = pl.BlockSpec((tm, PEER_SLOTS), lambda i: (i, 0))
    return pl.pallas_call(
        functools.partial(_slot_sort_kernel, tm=tm, part_shift=part_shift),
        grid=(T // tm,),
        in_specs=[in_spec, in_spec],
        out_specs=[out_spec, out_spec, out_spec],
        out_shape=[jax.ShapeDtypeStruct((T, PEER_SLOTS), I32), jax.ShapeDtypeStruct((T, PEER_SLOTS), F32),
                   jax.ShapeDtypeStruct((T, PEER_SLOTS), I32)],
        compiler_params=_cparams(("parallel",)),
        name="peer_slot_sort",
    )(e, g)


SUBLANES = 8
META_W = 2 * PEER_PARTS
_BITREV3 = (0, 4, 2, 6, 1, 5, 3, 7)


GROUP_TILES = 2
GROUP_SLOTS = GROUP_TILES * SUBLANES
_GROUP_SHIFT = GROUP_SLOTS.bit_length() - 1
_LAST_GROUP = PEER_SLOTS // GROUP_SLOTS - 1


def _slot_groups(meta_ref, t, p):
    start = meta_ref[t * META_W + p]
    cnt = meta_ref[t * META_W + PEER_PARTS + p]
    g0 = lax.shift_right_logical(start, _GROUP_SHIFT)
    g1 = jnp.where(cnt > 0, lax.shift_right_logical(start + cnt + (GROUP_SLOTS - 1), _GROUP_SHIFT), g0)
    return start, cnt, g0, g1


def _gather_rows(tab_ref, se_ref, t, g, order):
    base = t * PEER_SLOTS + g * GROUP_SLOTS
    return tuple(tab_ref[se_ref[base + tile * SUBLANES + i]] for tile in range(GROUP_TILES) for i in order)


def _fold_rows(a, b, first, sh):
    return jnp.where(first, a, b) + pltpu.roll(jnp.where(first, b, a), sh, axis=0)


def _fold_masks():
    sub = lax.broadcasted_iota(I32, (SUBLANES, LANES), 0)
    band = lambda lo: (sub >= lo) & (sub < lo + 4)
    return (band(0), band(2), band(1), band(3)), ((sub & 2) == 0, ((sub + 7) & 2) == 0), (sub & 1) == 0


def _sublane_totals(parts, masks):
    quarter, half, even = masks
    q = [_fold_rows(parts[2 * m], parts[2 * m + 1], quarter[m], 4) for m in range(4)]
    h = [_fold_rows(q[0], q[1], half[0], 6), _fold_rows(q[2], q[3], half[1], 6)]
    return _fold_rows(h[0], h[1], even, 7)


def _peer_down_kernel(u_ref, x_ref, se_ref, meta_ref, sg_ref, w_ref, r_scr, a_scr, *, tb):
    p = pl.program_id(0)
    lane1 = lax.broadcasted_iota(I32, (1, LANES), 1)

    @pl.when((p == 0) & (pl.program_id(1) == 0))
    def _():
        r_scr[...] = jnp.zeros_like(r_scr)

    def token(t, carry):
        _, _, g0, g1 = _slot_groups(meta_ref, t, p)
        xt = x_ref[t]

        def fold_group(g, rows):
            base = pl.multiple_of(t * PEER_SLOTS + g * GROUP_SLOTS, GROUP_SLOTS)
            for tile in range(GROUP_TILES):
                parts = []
                for k in range(SUBLANES):
                    pr = xt * rows[tile * SUBLANES + k].astype(F32)
                    parts.append(pr[0:8] + pr[8:16])
                r_scr[pl.ds(base + tile * SUBLANES, SUBLANES), :] = _sublane_totals(parts, _fold_masks())

        def group(g, rows):
            nxt = _gather_rows(u_ref, se_ref, t, jnp.minimum(g + 1, _LAST_GROUP), _BITREV3)
            fold_group(g, rows)
            return nxt

        lax.fori_loop(g0, g1, group, _gather_rows(u_ref, se_ref, t, jnp.minimum(g0, _LAST_GROUP), _BITREV3))
        return carry

    lax.fori_loop(0, tb, token, 0)

    def finish(b, carry):
        for tl in range(SUBLANES):
            t = b * SUBLANES + tl
            start = meta_ref[t * META_W + p]
            cnt = meta_ref[t * META_W + PEER_PARTS + p]
            r_t = r_scr[pl.ds(pl.multiple_of(t * PEER_SLOTS, PEER_SLOTS), PEER_SLOTS), :]
            a_row = jnp.sum(r_t.T, axis=0, keepdims=True)
            valid = (lane1 >= start) & (lane1 < start + cnt)
            a_scr[pl.ds(t, 1), :] = jnp.where(valid, a_row, 0.0)
        return carry

    lax.fori_loop(0, tb // SUBLANES, finish, 0)
    w_ref[...] = jax.nn.gelu(a_scr[...]) * sg_ref[...]


def _peer_up_kernel(v_ref, y_ref, se_ref, meta_ref, w_ref, o_ref, *, tb):
    p = pl.program_id(0)

    def token(t, carry):
        _, _, g0, g1 = _slot_groups(meta_ref, t, p)

        def group(g, acc):
            base = t * PEER_SLOTS + g * GROUP_SLOTS
            for i in range(GROUP_SLOTS):
                acc = acc + w_ref[base + i] * v_ref[se_ref[base + i]].astype(F32)
            return acc

        o_ref[t] = lax.fori_loop(g0, g1, group, y_ref[t])
        return carry

    lax.fori_loop(0, tb, token, 0)


def _peer_experts(xn3, resid3, se, sg, meta, down_tab, up_tab, tb=64):
    T = xn3.shape[0]
    E = down_tab.shape[0]
    part_size = E // PEER_PARTS
    nt = T // tb
    rows = xn3.shape[1]
    se_flat = se.reshape(T * PEER_SLOTS)
    meta_flat = meta[:, :META_W].reshape(T * META_W)
    tab_spec = pl.BlockSpec((part_size, rows, LANES), lambda p, i: (p, 0, 0), pipeline_mode=pl.Buffered(1))
    tok_spec = pl.BlockSpec((tb, rows, LANES), lambda p, i: (i, 0, 0))
    se_spec = pl.BlockSpec((tb * PEER_SLOTS,), lambda p, i: (i,), memory_space=pltpu.SMEM)
    meta_spec = pl.BlockSpec((tb * META_W,), lambda p, i: (i,), memory_space=pltpu.SMEM)
    w_parts = pl.pallas_call(
        functools.partial(_peer_down_kernel, tb=tb),
        grid=(PEER_PARTS, nt),
        in_specs=[tab_spec, tok_spec, se_spec, meta_spec, pl.BlockSpec((tb, PEER_SLOTS), lambda p, i: (i, 0))],
        out_specs=pl.BlockSpec((None, tb, PEER_SLOTS), lambda p, i: (p, i, 0)),
        out_shape=jax.ShapeDtypeStruct((PEER_PARTS, T, PEER_SLOTS), F32),
        scratch_shapes=[pltpu.VMEM((tb * PEER_SLOTS, LANES), F32), pltpu.VMEM((tb, PEER_SLOTS), F32)],
        compiler_params=_cparams(("arbitrary", "arbitrary")),
        name="peer_down",
    )(down_tab, xn3, se_flat, meta_flat, sg)
    w_flat = w_parts.reshape(PEER_PARTS * T * PEER_SLOTS)
    w_spec = pl.BlockSpec((tb * PEER_SLOTS,), lambda p, i: (p * nt + i,), memory_space=pltpu.SMEM)
    return pl.pallas_call(
        functools.partial(_peer_up_kernel, tb=tb),
        grid=(PEER_PARTS, nt),
        in_specs=[tab_spec, tok_spec, se_spec, meta_spec, w_spec],
        out_specs=tok_spec,
        out_shape=jax.ShapeDtypeStruct(resid3.shape, F32),
        input_output_aliases={1: 0},
        compiler_params=_cparams(("arbitrary", "arbitrary")),
        name="peer_up",
    )(up_tab, resid3, se_flat, meta_flat, w_flat)


def _peer(x2, norm_gain, w_query, sub_keys_1, sub_keys_2, expert_down, expert_up):
    T, D = x2.shape
    E = expert_down.shape[0]
    rows = D // LANES
    xn_bf16, xn = _rmsnorm(x2, norm_gain, (BF16, F32))
    e, g = _route(xn_bf16, w_query, sub_keys_1, sub_keys_2)
    part_shift = int(math.log2(E // PEER_PARTS))
    se, sg, meta = _slot_sort(e, g, part_shift)
    down_tab = expert_down.astype(BF16).reshape(E, rows, LANES)
    up_tab = expert_up.astype(BF16).reshape(E, rows, LANES)
    y = _peer_experts(xn.reshape(T, rows, LANES), x2.reshape(T, rows, LANES), se, sg, meta, down_tab, up_tab)
    return y.reshape(T, D)


def kernel(x, norm_mix_gain, w_in, nsa_gate_bias, fox_forget_bias, merge_gate_bias, k_cmp_pos, k_cmp_w1, k_cmp_w2, v_cmp_pos, v_cmp_w1, v_cmp_w2, w_up_nsa, w_up_fox, w_out, norm_ffn_gain, peer_w_query, peer_sub_keys_1, peer_sub_keys_2, peer_expert_down, peer_expert_up, norm_final_gain):
    B, S, D = x.shape
    x2 = x.reshape(B * S, D)
    for l in range(norm_mix_gain.shape[0]):
        x2 = _mixer(x2, B, S, norm_mix_gain[l], w_in[l], nsa_gate_bias[l], fox_forget_bias[l], merge_gate_bias[l],
                    k_cmp_pos[l], k_cmp_w1[l], k_cmp_w2[l], v_cmp_pos[l], v_cmp_w1[l], v_cmp_w2[l],
                    w_up_nsa[l], w_up_fox[l], w_out[l])
        x2 = _peer(x2, norm_ffn_gain[l], peer_w_query[l], peer_sub_keys_1[l], peer_sub_keys_2[l], peer_expert_down[l],
                   peer_expert_up[l])
    (out,) = _rmsnorm(x2, norm_final_gain, (F32,))
    return out.reshape(B, S, D)
```

```python
import functools
import math

import jax
import jax.numpy as jnp
import numpy as np
from jax import lax
from jax.experimental import pallas as pl
from jax.experimental.pallas import tpu as pltpu

F32 = jnp.float32
BF16 = jnp.bfloat16
I32 = jnp.int32

LANES = 128
HEAD_DIM = 128
ROPE_THETA = 10000.0
NORM_EPS = 1e-6
NEG_INF = -1e30
TINY = 1e-30
PAD_SCORE = -3e38

NSA_HEADS = 8
NSA_GROUPS = 2
NSA_HPG = NSA_HEADS // NSA_GROUPS
CMP_BLOCK = 32
CMP_STRIDE = 16
SEL_BLOCK = 64
SEL_TOPK = 16
FORCED_SCORE = 1e4
WINDOW = 512
FOX_HEADS = 8

PEER_HEADS = 8
PEER_N_KEYS = 128
PEER_HALF = 128
PEER_TOPK = 16
PEER_SLOTS = PEER_HEADS * PEER_TOPK
PEER_PARTS = 2

VMEM_LIMIT_BYTES = 56 * 1024 * 1024


def _cparams(sem):
    return pltpu.CompilerParams(dimension_semantics=sem, vmem_limit_bytes=VMEM_LIMIT_BYTES)


def _split3(x):
    hi = x.astype(BF16)
    r = x - hi.astype(F32)
    mid = r.astype(BF16)
    lo = (r - mid.astype(F32)).astype(BF16)
    return hi, mid, lo


def _dot(a, b):
    return jnp.dot(a, b, preferred_element_type=F32)


def _dot_nt(a, b):
    return lax.dot_general(a, b, (((1,), (1,)), ((), ())), preferred_element_type=F32)


def _dot3_right(x, m01):
    hi, mid, lo = _split3(x)
    return _dot(hi, m01) + _dot(mid, m01) + _dot(lo, m01)


def _dot3_left(m01, x):
    hi, mid, lo = _split3(x)
    return _dot(m01, hi) + _dot(m01, mid) + _dot(m01, lo)


def _masked_softmax(s, mask):
    s = jnp.where(mask, s, NEG_INF)
    m = jnp.max(s, axis=-1, keepdims=True)
    e = jnp.where(mask, jnp.exp(s - m), 0.0)
    return e / jnp.maximum(jnp.sum(e, axis=-1, keepdims=True), TINY)


def _rope_tile(x, cos_full, sin_signed):
    return x * cos_full + pltpu.roll(x, HEAD_DIM // 2, axis=1) * sin_signed


def _rmsnorm_kernel(x_ref, g_ref, *o_refs):
    x = x_ref[...]
    y = x * lax.rsqrt(jnp.mean(x * x, axis=-1, keepdims=True) + NORM_EPS) * g_ref[...]
    for o_ref in o_refs:
        o_ref[...] = y.astype(o_ref.dtype)


def _rmsnorm(x2, gain, out_dtypes, tm=512):
    T, D = x2.shape
    outs = pl.pallas_call(
        _rmsnorm_kernel,
        grid=(T // tm,),
        in_specs=[pl.BlockSpec((tm, D), lambda i: (i, 0)), pl.BlockSpec((1, D), lambda i: (0, 0))],
        out_specs=[pl.BlockSpec((tm, D), lambda i: (i, 0)) for _ in out_dtypes],
        out_shape=[jax.ShapeDtypeStruct((T, D), dt) for dt in out_dtypes],
        compiler_params=_cparams(("parallel",)),
        name="rmsnorm",
    )(x2, gain.reshape(1, D).astype(F32))
    return outs


def _mm_kernel(a_ref, b_ref, *rest, epilogue, scaled_blocks):
    o_ref = rest[-1]
    acc = _dot(a_ref[...], b_ref[...])
    if scaled_blocks is not None:
        j = pl.program_id(1)
        acc = acc * jnp.where((j >= scaled_blocks[0]) & (j < scaled_blocks[1]), HEAD_DIM ** -0.5, 1.0)
    if epilogue == "rope":
        cos_ref, sin_ref = rest[0], rest[1]
        c, s = cos_ref[...], sin_ref[...]
        for j in range(acc.shape[1] // HEAD_DIM):
            sl = slice(j * HEAD_DIM, (j + 1) * HEAD_DIM)
            o_ref[:, sl] = _rope_tile(acc[:, sl], c, s).astype(o_ref.dtype)
    elif epilogue == "sigmoid_bias":
        o_ref[...] = jax.nn.sigmoid(acc + rest[0][...]).astype(o_ref.dtype)
    elif epilogue == "residual":
        o_ref[...] = (rest[0][...] + acc).astype(o_ref.dtype)
    else:
        o_ref[...] = acc.astype(o_ref.dtype)


def _matmul(a, b, out_dtype, epilogue="none", extras=(), tm=1024, tn=512, seq=None, scaled_cols=None, name="matmul"):
    M, K = a.shape
    _, N = b.shape
    tn = min(tn, N)
    tm = min(tm, M)
    scaled_blocks = None if scaled_cols is None else (scaled_cols[0] // tn, scaled_cols[1] // tn)
    in_specs = [pl.BlockSpec((tm, K), lambda i, j: (i, 0)), pl.BlockSpec((K, tn), lambda i, j: (0, j))]
    if epilogue == "rope":
        nrep = seq // tm
        in_specs += [pl.BlockSpec((tm, HEAD_DIM), lambda i, j: (i % nrep, 0))] * 2
    elif epilogue == "sigmoid_bias":
        in_specs += [pl.BlockSpec((1, tn), lambda i, j: (0, j))]
    elif epilogue == "residual":
        in_specs += [pl.BlockSpec((tm, tn), lambda i, j: (i, j))]
    return pl.pallas_call(
        functools.partial(_mm_kernel, epilogue=epilogue, scaled_blocks=scaled_blocks),
        grid=(M // tm, N // tn),
        in_specs=in_specs,
        out_specs=pl.BlockSpec((tm, tn), lambda i, j: (i, j)),
        out_shape=jax.ShapeDtypeStruct((M, N), out_dtype),
        compiler_params=_cparams(("parallel", "parallel")),
        name=name,
    )(a, b, *extras)


def _compress_kernel(ch_ref, pos_ref, w1t_ref, w1b_ref, w2_ref, cos_ref, sin_ref, o_ref, *, rope):
    ch = ch_ref[...].astype(F32)
    a_top = (ch + pos_ref[0:1, :]).astype(BF16)
    a_bot = (ch + pos_ref[1:2, :]).astype(BF16)
    y_top = _dot(a_top, w1t_ref[...])
    y_bot = _dot(a_bot, w1b_ref[...])
    n = y_bot.shape[0]
    hidden = y_top + pltpu.roll(y_bot, n - 1, axis=0)
    out = _dot(jax.nn.gelu(hidden).astype(BF16), w2_ref[...])
    if rope:
        out = _rope_tile(out, cos_ref[...], sin_ref[...])
    row = lax.broadcasted_iota(I32, out.shape, 0)
    o_ref[...] = jnp.where(row < n - 1, out, 0.0).astype(o_ref.dtype)


def _compress(chunks, pos, w1, w2, cos_c, sin_c, rope):
    BG, NC, CK = chunks.shape
    hid = w1.shape[1]
    pos2 = pos.reshape(2, CK).astype(F32)
    return pl.pallas_call(
        functools.partial(_compress_kernel, rope=rope),
        grid=(BG,),
        in_specs=[
            pl.BlockSpec((None, NC, CK), lambda i: (i, 0, 0)),
            pl.BlockSpec((2, CK), lambda i: (0, 0)),
            pl.BlockSpec((CK, hid), lambda i: (0, 0)),
            pl.BlockSpec((CK, hid), lambda i: (1, 0)),
            pl.BlockSpec((hid, HEAD_DIM), lambda i: (0, 0)),
            pl.BlockSpec((NC, HEAD_DIM), lambda i: (0, 0)),
            pl.BlockSpec((NC, HEAD_DIM), lambda i: (0, 0)),
        ],
        out_specs=pl.BlockSpec((None, NC, HEAD_DIM), lambda i: (i, 0, 0)),
        out_shape=jax.ShapeDtypeStruct((BG, NC, HEAD_DIM), BF16),
        compiler_params=_cparams(("parallel",)),
        name="compress",
    )(chunks, pos2, w1.astype(BF16), w1.astype(BF16), w2.astype(BF16), cos_c, sin_c)


CAUSAL_SPANS = 4


def _causal_span(q_end, seq):
    return lax.shift_right_logical(q_end - 1, (seq // CAUSAL_SPANS).bit_length() - 1)


def _attend(q, k, v, mask):
    s = jnp.where(mask, _dot_nt(q, k), NEG_INF)
    e = jnp.exp(s - jnp.max(s, axis=-1, keepdims=True))
    return _dot(e.astype(BF16), v) * (1.0 / jnp.sum(e, axis=-1, keepdims=True))


def _nsa_kernel(q_ref, ks_ref, kw_ref, vs_ref, vw_ref, kc_ref, vc_ref, gate_ref, gbias_ref, ovl_ref, exp_ref,
                o_ref, osel_scr, *, tq, seq):
    g = pl.program_id(1)
    t0 = pl.program_id(2) * tq
    n_sel = seq // SEL_BLOCK
    n_cmp = seq // CMP_STRIDE - CMP_BLOCK // CMP_STRIDE + 1
    wlen = tq + WINDOW

    t_col = t0 + lax.broadcasted_iota(I32, (tq, 1), 0)
    lane = lax.broadcasted_iota(I32, (tq, LANES), 1)
    gate_all = jax.nn.sigmoid(gate_ref[...] + gbias_ref[...])

    def gate(h, j):
        cols = [gate_all[:, (gg * NSA_HPG + h) * 3 + j:(gg * NSA_HPG + h) * 3 + j + 1] for gg in range(NSA_GROUPS)]
        out = cols[0]
        for gg in range(1, NSA_GROUPS):
            out = jnp.where(g == gg, cols[gg], out)
        return out

    cmp_mask = (lane * CMP_STRIDE + (CMP_BLOCK - 1) <= t_col) & (lane < n_cmp)
    wstart = pl.multiple_of(jnp.maximum(t0 - WINDOW, 0), tq)
    key_w = wstart + lax.broadcasted_iota(I32, (tq, wlen), 1)
    win_mask = (key_w <= t_col) & (t_col - key_w < WINDOW)

    cur = t_col >> 6
    forced = (lane == 0) | (lane == cur) | (lane == cur - 1)
    future = lane * SEL_BLOCK > t_col

    heads = [slice(h * HEAD_DIM, (h + 1) * HEAD_DIM) for h in range(NSA_HPG)]
    kc = kc_ref[...]
    vc = vc_ref[...]
    o_cmp = []
    psum = jnp.zeros((tq, LANES), F32)
    for h in range(NSA_HPG):
        p = _masked_softmax(_dot_nt(q_ref[:, heads[h]], kc), cmp_mask)
        psum = psum + p
        o_cmp.append(_dot(p.astype(BF16), vc))
    imp = _dot3_right(psum, ovl_ref[...])
    score = jnp.where(forced, FORCED_SCORE, imp)
    score = jnp.where(future, NEG_INF, score)
    score = jnp.where(lane < n_sel, score, PAD_SCORE)
    rank = jnp.zeros((tq, LANES), I32)
    for k in range(n_sel):
        ck = score[:, k:k + 1]
        beats = (ck > score) | ((ck == score) & (lane > k))
        rank = rank + beats.astype(I32)
    sel = ((rank < min(SEL_TOPK, n_sel)) & (lane < n_sel)).astype(BF16)

    for span in range(CAUSAL_SPANS):
        klen = (span + 1) * (seq // CAUSAL_SPANS)

        @pl.when(_causal_span(t0 + tq, seq) == span)
        def _(klen=klen):
            causal = lax.broadcasted_iota(I32, (tq, klen), 1) <= t_col
            sel_mask = (_dot(sel, exp_ref[:, 0:klen]) > 0.5) & causal
            for h in range(NSA_HPG):
                osel_scr[:, heads[h]] = gate(h, 1) * _attend(q_ref[:, heads[h]], ks_ref[0:klen, :], vs_ref[0:klen, :], sel_mask)

    kw = kw_ref[pl.ds(wstart, wlen), :]
    vw = vw_ref[pl.ds(wstart, wlen), :]
    for h in range(NSA_HPG):
        o_win = _attend(q_ref[:, heads[h]], kw, vw, win_mask)
        o = gate(h, 0) * o_cmp[h] + osel_scr[:, heads[h]] + gate(h, 2) * o_win
        o_ref[:, heads[h]] = o.astype(o_ref.dtype)


def _nsa(rope_proj, plain_proj, kc, vc, small_proj, gate_bias, B, S, tq=256):
    n_sel = S // SEL_BLOCK
    n_cmp = S // CMP_STRIDE - CMP_BLOCK // CMP_STRIDE + 1
    c_start = np.arange(LANES) * CMP_STRIDE
    s_start = np.arange(LANES) * SEL_BLOCK
    ovl = (c_start[:, None] < s_start[None, :] + SEL_BLOCK) & (s_start[None, :] < c_start[:, None] + CMP_BLOCK)
    ovl &= (np.arange(LANES)[:, None] < n_cmp) & (np.arange(LANES)[None, :] < n_sel)
    expand = (np.arange(LANES)[:, None] == (np.arange(S)[None, :] // SEL_BLOCK))
    rp = rope_proj.reshape(B, S, -1)
    pp = plain_proj.reshape(B, S, -1)
    sm = small_proj.reshape(B, S, LANES)
    qw = NSA_HEADS * HEAD_DIM
    gqw = NSA_HPG * HEAD_DIM
    nq = qw // HEAD_DIM
    kv = lambda col0: pl.BlockSpec((None, S, HEAD_DIM), lambda b, g, i: (b, 0, col0 + g))
    cmp_spec = pl.BlockSpec((None, None, LANES, HEAD_DIM), lambda b, g, i: (b, g, 0, 0))
    return pl.pallas_call(
        functools.partial(_nsa_kernel, tq=tq, seq=S),
        grid=(B, NSA_GROUPS, S // tq),
        in_specs=[
            pl.BlockSpec((None, tq, gqw), lambda b, g, i: (b, i, g)),
            kv(nq), kv(nq + NSA_GROUPS), kv(2 * NSA_GROUPS), kv(3 * NSA_GROUPS),
            cmp_spec, cmp_spec,
            pl.BlockSpec((None, tq, LANES), lambda b, g, i: (b, i, 0)),
            pl.BlockSpec((1, LANES), lambda b, g, i: (0, 0)),
            pl.BlockSpec((LANES, LANES), lambda b, g, i: (0, 0)),
            pl.BlockSpec((LANES, S), lambda b, g, i: (0, 0)),
        ],
        out_specs=pl.BlockSpec((None, tq, gqw), lambda b, g, i: (b, i, g)),
        out_shape=jax.ShapeDtypeStruct((B, S, qw), BF16),
        scratch_shapes=[pltpu.VMEM((tq, gqw), F32)],
        compiler_params=_cparams(("parallel", "parallel", "parallel")),
        name="nsa",
    )(rp, rp, rp, pp, pp, kc, vc, sm, gate_bias, jnp.asarray(ovl, BF16), jnp.asarray(expand, BF16))


def _forget_cumsum_kernel(sm_ref, fbias_ref, tri_ref, ccol_ref, crow_ref, *, seq, lane0):
    x = sm_ref[...] + fbias_ref[...]
    lane = lax.broadcasted_iota(I32, x.shape, 1)
    log_f = jnp.minimum(x, 0.0) - jnp.log1p(jnp.exp(-jnp.abs(x)))
    log_f = jnp.where((lane >= lane0) & (lane < lane0 + FOX_HEADS), log_f, 0.0)
    carry = jnp.zeros((1, LANES), F32)
    blk = tri_ref.shape[0]
    for r in range(seq // blk):
        c = _dot3_left(tri_ref[...], log_f[r * blk:(r + 1) * blk, :]) + carry
        ccol_ref[r * blk:(r + 1) * blk, :] = c
        carry = c[blk - 1:blk, :]
    crow_ref[...] = ccol_ref[...].T[lane0:lane0 + FOX_HEADS, :]


def _forget_cumsum(small_proj, fbias, B, S, lane0):
    tri = np.tril(np.ones((LANES, LANES), np.float32))
    return pl.pallas_call(
        functools.partial(_forget_cumsum_kernel, seq=S, lane0=lane0),
        grid=(B,),
        in_specs=[
            pl.BlockSpec((None, S, LANES), lambda b: (b, 0, 0)),
            pl.BlockSpec((1, LANES), lambda b: (0, 0)),
            pl.BlockSpec((LANES, LANES), lambda b: (0, 0)),
        ],
        out_specs=[
            pl.BlockSpec((None, S, LANES), lambda b: (b, 0, 0)),
            pl.BlockSpec((None, FOX_HEADS, S), lambda b: (b, 0, 0)),
        ],
        out_shape=[jax.ShapeDtypeStruct((B, S, LANES), F32), jax.ShapeDtypeStruct((B, FOX_HEADS, S), F32)],
        compiler_params=_cparams(("parallel",)),
        name="forget_cumsum",
    )(small_proj.reshape(B, S, LANES), fbias, jnp.asarray(tri, BF16))


def _fox_kernel(q_ref, k_ref, v_ref, ccol_ref, crow_ref, o_ref, *, tq, seq, lane0):
    i = pl.program_id(1)
    t_col = i * tq + lax.broadcasted_iota(I32, (tq, 1), 0)
    for span in range(CAUSAL_SPANS):
        klen = (span + 1) * (seq // CAUSAL_SPANS)

        @pl.when(_causal_span((i + 1) * tq, seq) == span)
        def _(klen=klen):
            causal = lax.broadcasted_iota(I32, (tq, klen), 1) <= t_col
            for h in range(FOX_HEADS):
                hs = slice(h * HEAD_DIM, (h + 1) * HEAD_DIM)
                s = _dot_nt(q_ref[:, hs], k_ref[0:klen, hs])
                s = s + ccol_ref[:, lane0 + h:lane0 + h + 1] - crow_ref[h:h + 1, 0:klen]
                s = jnp.where(causal, s, NEG_INF)
                e = jnp.exp(s - jnp.max(s, axis=-1, keepdims=True))
                o = _dot(e.astype(BF16), v_ref[0:klen, hs]) * (1.0 / jnp.sum(e, axis=-1, keepdims=True))
                o_ref[:, hs] = o.astype(o_ref.dtype)


def _fox(plain_proj, ccol, crow, B, S, lane0, tq=256):
    pp = plain_proj.reshape(B, S, -1)
    fw = FOX_HEADS * HEAD_DIM
    return pl.pallas_call(
        functools.partial(_fox_kernel, tq=tq, seq=S, lane0=lane0),
        grid=(B, S // tq),
        in_specs=[
            pl.BlockSpec((None, tq, fw), lambda b, i: (b, i, 1)),
            pl.BlockSpec((None, S, fw), lambda b, i: (b, 0, 2)),
            pl.BlockSpec((None, S, fw), lambda b, i: (b, 0, 3)),
            pl.BlockSpec((None, tq, LANES), lambda b, i: (b, i, 0)),
            pl.BlockSpec((None, FOX_HEADS, S), lambda b, i: (b, 0, 0)),
        ],
        out_specs=pl.BlockSpec((None, tq, fw), lambda b, i: (b, i, 0)),
        out_shape=jax.ShapeDtypeStruct((B, S, fw), BF16),
        compiler_params=_cparams(("parallel", "parallel")),
        name="fox",
    )(pp, pp, pp, ccol, crow)


def _merge_kernel(on_ref, of_ref, wn_ref, wf_ref, ga_ref, gb_ref, o_ref):
    a = _dot(on_ref[...], wn_ref[...])
    b = _dot(of_ref[...], wf_ref[...])
    o_ref[...] = (ga_ref[...].astype(F32) * a + gb_ref[...].astype(F32) * b).astype(o_ref.dtype)


def _merge(o_nsa, o_fox, w_up_nsa, w_up_fox, gates, d_model, tm=1024, tn=512):
    T, K = o_nsa.shape
    nb = d_model // tn
    return pl.pallas_call(
        _merge_kernel,
        grid=(T // tm, nb),
        in_specs=[
            pl.BlockSpec((tm, K), lambda i, j: (i, 0)),
            pl.BlockSpec((tm, K), lambda i, j: (i, 0)),
            pl.BlockSpec((K, tn), lambda i, j: (0, j)),
            pl.BlockSpec((K, tn), lambda i, j: (0, j)),
            pl.BlockSpec((tm, tn), lambda i, j: (i, j)),
            pl.BlockSpec((tm, tn), lambda i, j: (i, j + nb)),
        ],
        out_specs=pl.BlockSpec((tm, tn), lambda i, j: (i, j)),
        out_shape=jax.ShapeDtypeStruct((T, d_model), BF16),
        compiler_params=_cparams(("parallel", "parallel")),
        name="merge",
    )(o_nsa, o_fox, w_up_nsa, w_up_fox, gates, gates)


def _rope_tables(pos):
    inv_freq = ROPE_THETA ** (-jnp.arange(0, HEAD_DIM, 2, dtype=F32) / HEAD_DIM)
    ang = pos.astype(F32)[:, None] * inv_freq[None, :]
    c, s = jnp.cos(ang), jnp.sin(ang)
    return jnp.concatenate([c, c], axis=-1), jnp.concatenate([-s, s], axis=-1)


def _mixer(x2, B, S, norm_gain, w_in, nsa_gate_bias, fox_forget_bias, merge_gate_bias, k_cmp_pos, k_cmp_w1, k_cmp_w2,
           v_cmp_pos, v_cmp_w1, v_cmp_w2, w_up_nsa, w_up_fox, w_out):
    T, D = x2.shape
    qw = NSA_HEADS * HEAD_DIM
    gw = NSA_GROUPS * HEAD_DIM
    fw = FOX_HEADS * HEAD_DIM
    ngate = 3 * NSA_HEADS
    widths = (qw, gw, gw, gw, gw, gw, gw, ngate, fw, fw, fw, FOX_HEADS, D, D)
    offs = np.concatenate([[0], np.cumsum(widths)])
    (w_qn, w_kc, w_vc, w_ks, w_vs, w_kw, w_vw, w_gn, w_qf, w_kf, w_vf, w_ff, w_ga, w_gb) = [
        w_in[:, offs[j]:offs[j + 1]] for j in range(len(widths))]
    w_rope = jnp.concatenate([w_qn, w_ks, w_kw], axis=1).astype(BF16)
    w_plain = jnp.concatenate([w_kc, w_vc, w_vs, w_vw, w_qf, w_kf, w_vf], axis=1).astype(BF16)
    w_gate = jnp.concatenate([w_ga, w_gb], axis=1).astype(BF16)
    pad = LANES - ngate - FOX_HEADS
    w_small = jnp.concatenate([w_gn, w_ff, jnp.zeros((D, pad), w_in.dtype)], axis=1).astype(BF16)
    gate_bias = jnp.concatenate([nsa_gate_bias, jnp.zeros((LANES - ngate,), F32)]).reshape(1, LANES)
    fbias = jnp.concatenate([jnp.zeros((ngate,), F32), fox_forget_bias, jnp.zeros((pad,), F32)]).reshape(1, LANES)

    (h,) = _rmsnorm(x2, norm_gain, (BF16,))
    cos_t, sin_t = _rope_tables(jnp.arange(S))
    rope_proj = _matmul(h, w_rope, BF16, "rope", (cos_t, sin_t), seq=S, scaled_cols=(0, qw), name="inproj_rope")
    plain_proj = _matmul(h, w_plain, BF16, scaled_cols=(4 * gw, 4 * gw + fw), name="inproj_plain")
    gates = _matmul(h, w_gate, BF16, "sigmoid_bias", (merge_gate_bias.reshape(1, 2 * D),), name="inproj_gate")
    small_proj = _matmul(h, w_small, F32, name="inproj_small")

    n_chunks = S // CMP_STRIDE
    ck = CMP_STRIDE * HEAD_DIM

    def chunks(col0):
        c = plain_proj[:, col0:col0 + gw].reshape(B, S, NSA_GROUPS, HEAD_DIM).transpose(0, 2, 1, 3)
        return c.reshape(B * NSA_GROUPS, n_chunks, ck)

    cos_c, sin_c = _rope_tables(jnp.arange(n_chunks) * CMP_STRIDE + CMP_BLOCK - 1)
    kc = _compress(chunks(0), k_cmp_pos, k_cmp_w1, k_cmp_w2, cos_c, sin_c, True)
    vc = _compress(chunks(gw), v_cmp_pos, v_cmp_w1, v_cmp_w2, cos_c, sin_c, False)
    kc = kc.reshape(B, NSA_GROUPS, n_chunks, HEAD_DIM)
    vc = vc.reshape(B, NSA_GROUPS, n_chunks, HEAD_DIM)

    o_nsa = _nsa(rope_proj, plain_proj, kc, vc, small_proj, gate_bias, B, S)
    ccol, crow = _forget_cumsum(small_proj, fbias, B, S, ngate)
    o_fox = _fox(plain_proj, ccol, crow, B, S, ngate)

    merged = _merge(o_nsa.reshape(T, qw), o_fox.reshape(T, fw), w_up_nsa.astype(BF16), w_up_fox.astype(BF16), gates, D)
    return _matmul(merged, w_out.astype(BF16), F32, "residual", (x2,), name="outproj")


def _candidate_pieces():
    pieces = [((0, 1), (0, PEER_TOPK))]
    pieces += [((k, k + 1), (0, SUBLANES)) for k in range(1, SUBLANES)]
    pieces += [((SUBLANES, PEER_TOPK), (0, 1))]
    covered = {(a, b) for (a0, a1), (b0, b1) in pieces for a in range(a0, a1) for b in range(b0, b1)}
    assert all((a, b) in covered for a in range(PEER_TOPK) for b in range(PEER_TOPK) if (a + 1) * (b + 1) <= PEER_TOPK)
    return pieces


def _candidate_flat_index(tm):
    rows = [a * PEER_TOPK + b for (a0, a1), (b0, b1) in _candidate_pieces() for a in range(a0, a1) for b in range(b0, b1)]
    return jnp.asarray(np.broadcast_to(np.asarray(rows, np.float32)[:, None], (len(rows), tm)))


def _route_kernel(xn_ref, wq_ref, k1_ref, k2_ref, flat_ref, e_ref, g_ref, *, tm):
    q = _dot(xn_ref[...], wq_ref[...])
    s1 = _dot_nt(k1_ref[...], q[:, :PEER_HALF].astype(BF16))
    s2 = _dot_nt(k2_ref[...], q[:, PEER_HALF:].astype(BF16))
    key_row = lax.broadcasted_iota(I32, (PEER_N_KEYS, tm), 0).astype(F32)
    rank_row = lax.broadcasted_iota(I32, (PEER_TOPK, tm), 0)
    cand_row = flat_ref[...]
    nc = float(PEER_TOPK * PEER_TOPK)

    def top_keys(s):
        vals = jnp.zeros((PEER_TOPK, tm), F32)
        idxs = jnp.zeros((PEER_TOPK, tm), F32)
        for r in range(PEER_TOPK):
            m = jnp.max(s, axis=0, keepdims=True)
            idx = jnp.min(jnp.where(s == m, key_row, float(PEER_N_KEYS)), axis=0, keepdims=True)
            vals = jnp.where(rank_row == r, m, vals)
            idxs = jnp.where(rank_row == r, idx, idxs)
            s = jnp.where(key_row == idx, -jnp.inf, s)
        return vals, idxs

    v1, i1 = top_keys(s1)
    v2, i2 = top_keys(s2)
    pieces = _candidate_pieces()
    cand = jnp.concatenate([v1[a0:a1, :] + v2[b0:b1, :] for (a0, a1), (b0, b1) in pieces], axis=0)
    cidx = jnp.concatenate([i1[a0:a1, :] * float(PEER_N_KEYS) + i2[b0:b1, :] for (a0, a1), (b0, b1) in pieces],
                           axis=0)

    top_s = jnp.zeros((PEER_TOPK, tm), F32)
    top_e = jnp.zeros((PEER_TOPK, tm), F32)
    for r in range(PEER_TOPK):
        m = jnp.max(cand, axis=0, keepdims=True)
        pos = jnp.min(jnp.where(cand == m, cand_row, nc), axis=0, keepdims=True)
        hit = cand_row == pos
        e = jnp.sum(jnp.where(hit, cidx, 0.0), axis=0, keepdims=True)
        top_s = jnp.where(rank_row == r, m, top_s)
        top_e = jnp.where(rank_row == r, e, top_e)
        cand = jnp.where(hit, -jnp.inf, cand)

    ex = jnp.exp(top_s - top_s[0:1, :])
    e_ref[...] = top_e.astype(I32)
    g_ref[...] = ex / jnp.sum(ex, axis=0, keepdims=True)


def _route(xn_bf16, w_query, sub_keys_1, sub_keys_2, tm=1024):
    T, D = xn_bf16.shape
    kd = 2 * PEER_HALF
    flat = _candidate_flat_index(tm)
    return pl.pallas_call(
        functools.partial(_route_kernel, tm=tm),
        grid=(T // tm, PEER_HEADS),
        in_specs=[
            pl.BlockSpec((tm, D), lambda i, h: (i, 0)),
            pl.BlockSpec((D, kd), lambda i, h: (0, h)),
            pl.BlockSpec((None, PEER_N_KEYS, PEER_HALF), lambda i, h: (h, 0, 0)),
            pl.BlockSpec((None, PEER_N_KEYS, PEER_HALF), lambda i, h: (h, 0, 0)),
            pl.BlockSpec(flat.shape, lambda i, h: (0, 0)),
        ],
        out_specs=[pl.BlockSpec((PEER_TOPK, tm), lambda i, h: (h, i))] * 2,
        out_shape=[jax.ShapeDtypeStruct((PEER_SLOTS, T), I32), jax.ShapeDtypeStruct((PEER_SLOTS, T), F32)],
        compiler_params=_cparams(("parallel", "parallel")),
        name="peer_route",
    )(xn_bf16, w_query.astype(BF16), sub_keys_1.astype(BF16), sub_keys_2.astype(BF16), flat)


def _slot_sort_kernel(e_ref, g_ref, se_ref, sg_ref, meta_ref, *, tm, part_shift):
    row = lax.broadcasted_iota(I32, (PEER_SLOTS, tm), 0)
    e = e_ref[...]
    g = g_ref[...]
    key = e * PEER_SLOTS + row
    rank = jnp.zeros((PEER_SLOTS, tm), I32)
    for k in range(PEER_SLOTS):
        rank = rank + (key[k:k + 1, :] < key).astype(I32)
    se = jnp.zeros((PEER_SLOTS, tm), I32)
    sg = jnp.zeros((PEER_SLOTS, tm), F32)
    for k in range(PEER_SLOTS):
        hit = rank[k:k + 1, :] == row
        se = jnp.where(hit, e[k:k + 1, :], se)
        sg = jnp.where(hit, g[k:k + 1, :], sg)
    part = se >> part_shift
    meta = jnp.zeros((PEER_SLOTS, tm), I32)
    start = jnp.zeros((1, tm), I32)
    for p in range(PEER_PARTS):
        cnt = jnp.sum((part == p).astype(I32), axis=0, keepdims=True)
        meta = jnp.where(row == p, start, meta)
        meta = jnp.where(row == PEER_PARTS + p, cnt, meta)
        start = start + cnt
    se_ref[...] = (se & ((1 << part_shift) - 1)).T
    sg_ref[...] = sg.T
    meta_ref[...] = meta.T


def _slot_sort(e, g, part_shift, tm=128):
    T = e.shape[1]
    in_spec = pl.BlockSpec((PEER_SLOTS, tm), lambda i: (0, i))
    out_spec = pl.BlockSpec((tm, PEER_SLOTS), lambda i: (i, 0))
    return pl.pallas_call(
        functools.partial(_slot_sort_kernel, tm=tm, part_shift=part_shift),
        grid=(T // tm,),
        in_specs=[in_spec, in_spec],
        out_specs=[out_spec, out_spec, out_spec],
        out_shape=[jax.ShapeDtypeStruct((T, PEER_SLOTS), I32), jax.ShapeDtypeStruct((T, PEER_SLOTS), F32),
                   jax.ShapeDtypeStruct((T, PEER_SLOTS), I32)],
        compiler_params=_cparams(("parallel",)),
        name="peer_slot_sort",
    )(e, g)


SUBLANES = 8
META_W = 2 * PEER_PARTS
_BITREV3 = (0, 4, 2, 6, 1, 5, 3, 7)


DOWN_GROUP = 32
UP_GROUP = 32


def _slot_groups(meta_ref, t, p, group):
    shift = group.bit_length() - 1
    start = meta_ref[t * META_W + p]
    cnt = meta_ref[t * META_W + PEER_PARTS + p]
    g0 = lax.shift_right_logical(start, shift)
    g1 = jnp.where(cnt > 0, lax.shift_right_logical(start + cnt + (group - 1), shift), g0)
    return g0, g1


def _fold_rows(a, b, first, sh):
    return jnp.where(first, a, b) + pltpu.roll(jnp.where(first, b, a), sh, axis=0)


def _fold_masks():
    sub = lax.broadcasted_iota(I32, (SUBLANES, LANES), 0)
    band = lambda lo: (sub >= lo) & (sub < lo + 4)
    return (band(0), band(2), band(1), band(3)), ((sub & 2) == 0, ((sub + 7) & 2) == 0), (sub & 1) == 0


def _sublane_totals(parts, masks):
    quarter, half, even = masks
    q = [_fold_rows(parts[2 * m], parts[2 * m + 1], quarter[m], 4) for m in range(4)]
    h = [_fold_rows(q[0], q[1], half[0], 6), _fold_rows(q[2], q[3], half[1], 6)]
    return _fold_rows(h[0], h[1], even, 7)


def _peer_down_kernel(u_ref, x_ref, se_ref, meta_ref, sg_ref, w_ref, r_scr, a_scr, *, tb):
    p = pl.program_id(0)
    lane1 = lax.broadcasted_iota(I32, (1, LANES), 1)

    @pl.when((p == 0) & (pl.program_id(1) == 0))
    def _():
        r_scr[...] = jnp.zeros_like(r_scr)

    def token(t, carry):
        g0, g1 = _slot_groups(meta_ref, t, p, DOWN_GROUP)
        xt = x_ref[t]

        def group(g, c2):
            base = pl.multiple_of(t * PEER_SLOTS + g * DOWN_GROUP, DOWN_GROUP)
            for tile in range(0, DOWN_GROUP, SUBLANES):
                parts = []
                for k in range(SUBLANES):
                    pr = xt * u_ref[se_ref[base + tile + _BITREV3[k]]].astype(F32)
                    parts.append(pr[0:8] + pr[8:16])
                r_scr[pl.ds(base + tile, SUBLANES), :] = _sublane_totals(parts, _fold_masks())
            return c2

        lax.fori_loop(g0, g1, group, 0)
        return carry

    lax.fori_loop(0, tb, token, 0)

    def finish(b, carry):
        for tl in range(SUBLANES):
            t = b * SUBLANES + tl
            start = meta_ref[t * META_W + p]
            cnt = meta_ref[t * META_W + PEER_PARTS + p]
            r_t = r_scr[pl.ds(pl.multiple_of(t * PEER_SLOTS, PEER_SLOTS), PEER_SLOTS), :]
            a_row = jnp.sum(r_t.T, axis=0, keepdims=True)
            valid = (lane1 >= start) & (lane1 < start + cnt)
            a_scr[pl.ds(t, 1), :] = jnp.where(valid, a_row, 0.0)
        return carry

    lax.fori_loop(0, tb // SUBLANES, finish, 0)
    w_ref[...] = jax.nn.gelu(a_scr[...]) * sg_ref[...]


def _peer_up_kernel(v_ref, y_ref, se_ref, meta_ref, w_ref, o_ref, *, tb):
    p = pl.program_id(0)

    def token(t, carry):
        g0, g1 = _slot_groups(meta_ref, t, p, UP_GROUP)

        def group(g, acc):
            base = t * PEER_SLOTS + g * UP_GROUP
            for i in range(UP_GROUP):
                acc = acc + w_ref[base + i] * v_ref[se_ref[base + i]].astype(F32)
            return acc

        o_ref[t] = lax.fori_loop(g0, g1, group, y_ref[t])
        return carry

    lax.fori_loop(0, tb, token, 0)


def _peer_experts(xn3, resid3, se, sg, meta, down_tab, up_tab, tb=128):
    T = xn3.shape[0]
    E = down_tab.shape[0]
    part_size = E // PEER_PARTS
    nt = T // tb
    rows = xn3.shape[1]
    se_flat = se.reshape(T * PEER_SLOTS)
    meta_flat = meta[:, :META_W].reshape(T * META_W)
    tab_spec = pl.BlockSpec((part_size, rows, LANES), lambda p, i: (p, 0, 0), pipeline_mode=pl.Buffered(1))
    tok_spec = pl.BlockSpec((tb, rows, LANES), lambda p, i: (i, 0, 0))
    se_spec = pl.BlockSpec((tb * PEER_SLOTS,), lambda p, i: (i,), memory_space=pltpu.SMEM)
    meta_spec = pl.BlockSpec((tb * META_W,), lambda p, i: (i,), memory_space=pltpu.SMEM)
    w_parts = pl.pallas_call(
        functools.partial(_peer_down_kernel, tb=tb),
        grid=(PEER_PARTS, nt),
        in_specs=[tab_spec, tok_spec, se_spec, meta_spec, pl.BlockSpec((tb, PEER_SLOTS), lambda p, i: (i, 0))],
        out_specs=pl.BlockSpec((None, tb, PEER_SLOTS), lambda p, i: (p, i, 0)),
        out_shape=jax.ShapeDtypeStruct((PEER_PARTS, T, PEER_SLOTS), F32),
        scratch_shapes=[pltpu.VMEM((tb * PEER_SLOTS, LANES), F32), pltpu.VMEM((tb, PEER_SLOTS), F32)],
        compiler_params=_cparams(("arbitrary", "arbitrary")),
        name="peer_down",
    )(down_tab, xn3, se_flat, meta_flat, sg)
    w_flat = w_parts.reshape(PEER_PARTS * T * PEER_SLOTS)
    w_spec = pl.BlockSpec((tb * PEER_SLOTS,), lambda p, i: (p * nt + i,), memory_space=pltpu.SMEM)
    return pl.pallas_call(
        functools.partial(_peer_up_kernel, tb=tb),
        grid=(PEER_PARTS, nt),
        in_specs=[tab_spec, tok_spec, se_spec, meta_spec, w_spec],
        out_specs=tok_spec,
        out_shape=jax.ShapeDtypeStruct(resid3.shape, F32),
        input_output_aliases={1: 0},
        compiler_params=_cparams(("arbitrary", "arbitrary")),
        name="peer_up",
    )(up_tab, resid3, se_flat, meta_flat, w_flat)


def _peer(x2, norm_gain, w_query, sub_keys_1, sub_keys_2, expert_down, expert_up):
    T, D = x2.shape
    E = expert_down.shape[0]
    rows = D // LANES
    xn_bf16, xn = _rmsnorm(x2, norm_gain, (BF16, F32))
    e, g = _route(xn_bf16, w_query, sub_keys_1, sub_keys_2)
    part_shift = int(math.log2(E // PEER_PARTS))
    se, sg, meta = _slot_sort(e, g, part_shift)
    down_tab = expert_down.astype(BF16).reshape(E, rows, LANES)
    up_tab = expert_up.astype(BF16).reshape(E, rows, LANES)
    y = _peer_experts(xn.reshape(T, rows, LANES), x2.reshape(T, rows, LANES), se, sg, meta, down_tab, up_tab)
    return y.reshape(T, D)


def kernel(x, norm_mix_gain, w_in, nsa_gate_bias, fox_forget_bias, merge_gate_bias, k_cmp_pos, k_cmp_w1, k_cmp_w2, v_cmp_pos, v_cmp_w1, v_cmp_w2, w_up_nsa, w_up_fox, w_out, norm_ffn_gain, peer_w_query, peer_sub_keys_1, peer_sub_keys_2, peer_expert_down, peer_expert_up, norm_final_gain):
    B, S, D = x.shape
    x2 = x.reshape(B * S, D)
    for l in range(norm_mix_gain.shape[0]):
        x2 = _mixer(x2, B, S, norm_mix_gain[l], w_in[l], nsa_gate_bias[l], fox_forget_bias[l], merge_gate_bias[l],
                    k_cmp_pos[l], k_cmp_w1[l], k_cmp_w2[l], v_cmp_pos[l], v_cmp_w1[l], v_cmp_w2[l],
                    w_up_nsa[l], w_up_fox[l], w_out[l])
        x2 = _peer(x2, norm_ffn_gain[l], peer_w_query[l], peer_sub_keys_1[l], peer_sub_keys_2[l], peer_expert_down[l],
                   peer_expert_up[l])
    (out,) = _rmsnorm(x2, norm_final_gain, (F32,))
    return out.reshape(B, S, D)
```

```python
import functools
import math

import jax
import jax.numpy as jnp
import numpy as np
from jax import lax
from jax.experimental import pallas as pl
from jax.experimental.pallas import tpu as pltpu

F32 = jnp.float32
BF16 = jnp.bfloat16
I32 = jnp.int32

LANES = 128
HEAD_DIM = 128
ROPE_THETA = 10000.0
NORM_EPS = 1e-6
NEG_INF = -1e30
TINY = 1e-30
PAD_SCORE = -3e38

NSA_HEADS = 8
NSA_GROUPS = 2
NSA_HPG = NSA_HEADS // NSA_GROUPS
CMP_BLOCK = 32
CMP_STRIDE = 16
SEL_BLOCK = 64
SEL_TOPK = 16
FORCED_SCORE = 1e4
WINDOW = 512
FOX_HEADS = 8

PEER_HEADS = 8
PEER_N_KEYS = 128
PEER_HALF = 128
PEER_TOPK = 16
PEER_SLOTS = PEER_HEADS * PEER_TOPK
PEER_PARTS = 2

VMEM_LIMIT_BYTES = 56 * 1024 * 1024


def _cparams(sem):
    return pltpu.CompilerParams(dimension_semantics=sem, vmem_limit_bytes=VMEM_LIMIT_BYTES)


def _split3(x):
    hi = x.astype(BF16)
    r = x - hi.astype(F32)
    mid = r.astype(BF16)
    lo = (r - mid.astype(F32)).astype(BF16)
    return hi, mid, lo


def _dot(a, b):
    return jnp.dot(a, b, preferred_element_type=F32)


def _dot_nt(a, b):
    return lax.dot_general(a, b, (((1,), (1,)), ((), ())), preferred_element_type=F32)


def _dot3_right(x, m01):
    hi, mid, lo = _split3(x)
    return _dot(hi, m01) + _dot(mid, m01) + _dot(lo, m01)


def _dot3_left(m01, x):
    hi, mid, lo = _split3(x)
    return _dot(m01, hi) + _dot(m01, mid) + _dot(m01, lo)


def _masked_softmax(s, mask):
    s = jnp.where(mask, s, NEG_INF)
    m = jnp.max(s, axis=-1, keepdims=True)
    e = jnp.where(mask, jnp.exp(s - m), 0.0)
    return e / jnp.maximum(jnp.sum(e, axis=-1, keepdims=True), TINY)


def _rope_tile(x, cos_full, sin_signed):
    return x * cos_full + pltpu.roll(x, HEAD_DIM // 2, axis=1) * sin_signed


def _rmsnorm_kernel(x_ref, g_ref, *o_refs):
    x = x_ref[...]
    y = x * lax.rsqrt(jnp.mean(x * x, axis=-1, keepdims=True) + NORM_EPS) * g_ref[...]
    for o_ref in o_refs:
        o_ref[...] = y.astype(o_ref.dtype)


def _rmsnorm(x2, gain, out_dtypes, tm=512):
    T, D = x2.shape
    outs = pl.pallas_call(
        _rmsnorm_kernel,
        grid=(T // tm,),
        in_specs=[pl.BlockSpec((tm, D), lambda i: (i, 0)), pl.BlockSpec((1, D), lambda i: (0, 0))],
        out_specs=[pl.BlockSpec((tm, D), lambda i: (i, 0)) for _ in out_dtypes],
        out_shape=[jax.ShapeDtypeStruct((T, D), dt) for dt in out_dtypes],
        compiler_params=_cparams(("parallel",)),
        name="rmsnorm",
    )(x2, gain.reshape(1, D).astype(F32))
    return outs


def _mm_kernel(a_ref, b_ref, *rest, epilogue, scaled_blocks):
    o_ref = rest[-1]
    acc = _dot(a_ref[...], b_ref[...])
    if scaled_blocks is not None:
        j = pl.program_id(1)
        acc = acc * jnp.where((j >= scaled_blocks[0]) & (j < scaled_blocks[1]), HEAD_DIM ** -0.5, 1.0)
    if epilogue == "rope":
        cos_ref, sin_ref = rest[0], rest[1]
        c, s = cos_ref[...], sin_ref[...]
        for j in range(acc.shape[1] // HEAD_DIM):
            sl = slice(j * HEAD_DIM, (j + 1) * HEAD_DIM)
            o_ref[:, sl] = _rope_tile(acc[:, sl], c, s).astype(o_ref.dtype)
    elif epilogue == "sigmoid_bias":
        o_ref[...] = jax.nn.sigmoid(acc + rest[0][...]).astype(o_ref.dtype)
    elif epilogue == "residual":
        o_ref[...] = (rest[0][...] + acc).astype(o_ref.dtype)
    else:
        o_ref[...] = acc.astype(o_ref.dtype)


def _matmul(a, b, out_dtype, epilogue="none", extras=(), tm=1024, tn=512, seq=None, scaled_cols=None, name="matmul"):
    M, K = a.shape
    _, N = b.shape
    tn = min(tn, N)
    tm = min(tm, M)
    scaled_blocks = None if scaled_cols is None else (scaled_cols[0] // tn, scaled_cols[1] // tn)
    in_specs = [pl.BlockSpec((tm, K), lambda i, j: (i, 0)), pl.BlockSpec((K, tn), lambda i, j: (0, j))]
    if epilogue == "rope":
        nrep = seq // tm
        in_specs += [pl.BlockSpec((tm, HEAD_DIM), lambda i, j: (i % nrep, 0))] * 2
    elif epilogue == "sigmoid_bias":
        in_specs += [pl.BlockSpec((1, tn), lambda i, j: (0, j))]
    elif epilogue == "residual":
        in_specs += [pl.BlockSpec((tm, tn), lambda i, j: (i, j))]
    return pl.pallas_call(
        functools.partial(_mm_kernel, epilogue=epilogue, scaled_blocks=scaled_blocks),
        grid=(M // tm, N // tn),
        in_specs=in_specs,
        out_specs=pl.BlockSpec((tm, tn), lambda i, j: (i, j)),
        out_shape=jax.ShapeDtypeStruct((M, N), out_dtype),
        compiler_params=_cparams(("parallel", "parallel")),
        name=name,
    )(a, b, *extras)


def _compress_kernel(ch_ref, pos_ref, w1t_ref, w1b_ref, w2_ref, cos_ref, sin_ref, o_ref, *, rope):
    ch = ch_ref[...].astype(F32)
    a_top = (ch + pos_ref[0:1, :]).astype(BF16)
    a_bot = (ch + pos_ref[1:2, :]).astype(BF16)
    y_top = _dot(a_top, w1t_ref[...])
    y_bot = _dot(a_bot, w1b_ref[...])
    n = y_bot.shape[0]
    hidden = y_top + pltpu.roll(y_bot, n - 1, axis=0)
    out = _dot(jax.nn.gelu(hidden).astype(BF16), w2_ref[...])
    if rope:
        out = _rope_tile(out, cos_ref[...], sin_ref[...])
    row = lax.broadcasted_iota(I32, out.shape, 0)
    o_ref[...] = jnp.where(row < n - 1, out, 0.0).astype(o_ref.dtype)


def _compress(chunks, pos, w1, w2, cos_c, sin_c, rope):
    BG, NC, CK = chunks.shape
    hid = w1.shape[1]
    pos2 = pos.reshape(2, CK).astype(F32)
    return pl.pallas_call(
        functools.partial(_compress_kernel, rope=rope),
        grid=(BG,),
        in_specs=[
            pl.BlockSpec((None, NC, CK), lambda i: (i, 0, 0)),
            pl.BlockSpec((2, CK), lambda i: (0, 0)),
            pl.BlockSpec((CK, hid), lambda i: (0, 0)),
            pl.BlockSpec((CK, hid), lambda i: (1, 0)),
            pl.BlockSpec((hid, HEAD_DIM), lambda i: (0, 0)),
            pl.BlockSpec((NC, HEAD_DIM), lambda i: (0, 0)),
            pl.BlockSpec((NC, HEAD_DIM), lambda i: (0, 0)),
        ],
        out_specs=pl.BlockSpec((None, NC, HEAD_DIM), lambda i: (i, 0, 0)),
        out_shape=jax.ShapeDtypeStruct((BG, NC, HEAD_DIM), BF16),
        compiler_params=_cparams(("parallel",)),
        name="compress",
    )(chunks, pos2, w1.astype(BF16), w1.astype(BF16), w2.astype(BF16), cos_c, sin_c)


CAUSAL_SPANS = 4


def _causal_span(q_end, seq):
    return lax.shift_right_logical(q_end - 1, (seq // CAUSAL_SPANS).bit_length() - 1)


def _attend(q, k, v, mask):
    s = jnp.where(mask, _dot_nt(q, k), NEG_INF)
    e = jnp.exp(s - jnp.max(s, axis=-1, keepdims=True))
    return _dot(e.astype(BF16), v) * (1.0 / jnp.sum(e, axis=-1, keepdims=True))


def _nsa_kernel(q_ref, ks_ref, kw_ref, vs_ref, vw_ref, kc_ref, vc_ref, gate_ref, gbias_ref, ovl_ref, exp_ref,
                o_ref, osel_scr, *, tq, seq):
    g = pl.program_id(1)
    t0 = pl.program_id(2) * tq
    n_sel = seq // SEL_BLOCK
    n_cmp = seq // CMP_STRIDE - CMP_BLOCK // CMP_STRIDE + 1
    wlen = tq + WINDOW

    t_col = t0 + lax.broadcasted_iota(I32, (tq, 1), 0)
    lane = lax.broadcasted_iota(I32, (tq, LANES), 1)
    gate_all = jax.nn.sigmoid(gate_ref[...] + gbias_ref[...])

    def gate(h, j):
        cols = [gate_all[:, (gg * NSA_HPG + h) * 3 + j:(gg * NSA_HPG + h) * 3 + j + 1] for gg in range(NSA_GROUPS)]
        out = cols[0]
        for gg in range(1, NSA_GROUPS):
            out = jnp.where(g == gg, cols[gg], out)
        return out

    cmp_mask = (lane * CMP_STRIDE + (CMP_BLOCK - 1) <= t_col) & (lane < n_cmp)
    wstart = pl.multiple_of(jnp.maximum(t0 - WINDOW, 0), tq)
    key_w = wstart + lax.broadcasted_iota(I32, (tq, wlen), 1)
    win_mask = (key_w <= t_col) & (t_col - key_w < WINDOW)

    cur = t_col >> 6
    forced = (lane == 0) | (lane == cur) | (lane == cur - 1)
    future = lane * SEL_BLOCK > t_col

    heads = [slice(h * HEAD_DIM, (h + 1) * HEAD_DIM) for h in range(NSA_HPG)]
    kc = kc_ref[...]
    vc = vc_ref[...]
    o_cmp = []
    psum = jnp.zeros((tq, LANES), F32)
    for h in range(NSA_HPG):
        p = _masked_softmax(_dot_nt(q_ref[:, heads[h]], kc), cmp_mask)
        psum = psum + p
        o_cmp.append(_dot(p.astype(BF16), vc))
    imp = _dot3_right(psum, ovl_ref[...])
    score = jnp.where(forced, FORCED_SCORE, imp)
    score = jnp.where(future, NEG_INF, score)
    score = jnp.where(lane < n_sel, score, PAD_SCORE)
    rank = jnp.zeros((tq, LANES), I32)
    for k in range(n_sel):
        ck = score[:, k:k + 1]
        beats = (ck > score) | ((ck == score) & (lane > k))
        rank = rank + beats.astype(I32)
    sel = ((rank < min(SEL_TOPK, n_sel)) & (lane < n_sel)).astype(BF16)

    for span in range(CAUSAL_SPANS):
        klen = (span + 1) * (seq // CAUSAL_SPANS)

        @pl.when(_causal_span(t0 + tq, seq) == span)
        def _(klen=klen):
            causal = lax.broadcasted_iota(I32, (tq, klen), 1) <= t_col
            sel_mask = (_dot(sel, exp_ref[:, 0:klen]) > 0.5) & causal
            for h in range(NSA_HPG):
                osel_scr[:, heads[h]] = gate(h, 1) * _attend(q_ref[:, heads[h]], ks_ref[0:klen, :], vs_ref[0:klen, :], sel_mask)

    kw = kw_ref[pl.ds(wstart, wlen), :]
    vw = vw_ref[pl.ds(wstart, wlen), :]
    for h in range(NSA_HPG):
        o_win = _attend(q_ref[:, heads[h]], kw, vw, win_mask)
        o = gate(h, 0) * o_cmp[h] + osel_scr[:, heads[h]] + gate(h, 2) * o_win
        o_ref[:, heads[h]] = o.astype(o_ref.dtype)


def _nsa(rope_proj, plain_proj, kc, vc, small_proj, gate_bias, B, S, tq=256):
    n_sel = S // SEL_BLOCK
    n_cmp = S // CMP_STRIDE - CMP_BLOCK // CMP_STRIDE + 1
    c_start = np.arange(LANES) * CMP_STRIDE
    s_start = np.arange(LANES) * SEL_BLOCK
    ovl = (c_start[:, None] < s_start[None, :] + SEL_BLOCK) & (s_start[None, :] < c_start[:, None] + CMP_BLOCK)
    ovl &= (np.arange(LANES)[:, None] < n_cmp) & (np.arange(LANES)[None, :] < n_sel)
    expand = (np.arange(LANES)[:, None] == (np.arange(S)[None, :] // SEL_BLOCK))
    rp = rope_proj.reshape(B, S, -1)
    pp = plain_proj.reshape(B, S, -1)
    sm = small_proj.reshape(B, S, LANES)
    qw = NSA_HEADS * HEAD_DIM
    gqw = NSA_HPG * HEAD_DIM
    nq = qw // HEAD_DIM
    kv = lambda col0: pl.BlockSpec((None, S, HEAD_DIM), lambda b, g, i: (b, 0, col0 + g))
    cmp_spec = pl.BlockSpec((None, None, LANES, HEAD_DIM), lambda b, g, i: (b, g, 0, 0))
    return pl.pallas_call(
        functools.partial(_nsa_kernel, tq=tq, seq=S),
        grid=(B, NSA_GROUPS, S // tq),
        in_specs=[
            pl.BlockSpec((None, tq, gqw), lambda b, g, i: (b, i, g)),
            kv(nq), kv(nq + NSA_GROUPS), kv(2 * NSA_GROUPS), kv(3 * NSA_GROUPS),
            cmp_spec, cmp_spec,
            pl.BlockSpec((None, tq, LANES), lambda b, g, i: (b, i, 0)),
            pl.BlockSpec((1, LANES), lambda b, g, i: (0, 0)),
            pl.BlockSpec((LANES, LANES), lambda b, g, i: (0, 0)),
            pl.BlockSpec((LANES, S), lambda b, g, i: (0, 0)),
        ],
        out_specs=pl.BlockSpec((None, tq, gqw), lambda b, g, i: (b, i, g)),
        out_shape=jax.ShapeDtypeStruct((B, S, qw), BF16),
        scratch_shapes=[pltpu.VMEM((tq, gqw), F32)],
        compiler_params=_cparams(("parallel", "parallel", "parallel")),
        name="nsa",
    )(rp, rp, rp, pp, pp, kc, vc, sm, gate_bias, jnp.asarray(ovl, BF16), jnp.asarray(expand, BF16))


def _forget_cumsum_kernel(sm_ref, fbias_ref, tri_ref, ccol_ref, crow_ref, *, seq, lane0):
    x = sm_ref[...] + fbias_ref[...]
    lane = lax.broadcasted_iota(I32, x.shape, 1)
    log_f = jnp.minimum(x, 0.0) - jnp.log1p(jnp.exp(-jnp.abs(x)))
    log_f = jnp.where((lane >= lane0) & (lane < lane0 + FOX_HEADS), log_f, 0.0)
    carry = jnp.zeros((1, LANES), F32)
    blk = tri_ref.shape[0]
    for r in range(seq // blk):
        c = _dot3_left(tri_ref[...], log_f[r * blk:(r + 1) * blk, :]) + carry
        ccol_ref[r * blk:(r + 1) * blk, :] = c
        carry = c[blk - 1:blk, :]
    crow_ref[...] = ccol_ref[...].T[lane0:lane0 + FOX_HEADS, :]


def _forget_cumsum(small_proj, fbias, B, S, lane0):
    tri = np.tril(np.ones((LANES, LANES), np.float32))
    return pl.pallas_call(
        functools.partial(_forget_cumsum_kernel, seq=S, lane0=lane0),
        grid=(B,),
        in_specs=[
            pl.BlockSpec((None, S, LANES), lambda b: (b, 0, 0)),
            pl.BlockSpec((1, LANES), lambda b: (0, 0)),
            pl.BlockSpec((LANES, LANES), lambda b: (0, 0)),
        ],
        out_specs=[
            pl.BlockSpec((None, S, LANES), lambda b: (b, 0, 0)),
            pl.BlockSpec((None, FOX_HEADS, S), lambda b: (b, 0, 0)),
        ],
        out_shape=[jax.ShapeDtypeStruct((B, S, LANES), F32), jax.ShapeDtypeStruct((B, FOX_HEADS, S), F32)],
        compiler_params=_cparams(("parallel",)),
        name="forget_cumsum",
    )(small_proj.reshape(B, S, LANES), fbias, jnp.asarray(tri, BF16))


def _fox_kernel(q_ref, k_ref, v_ref, ccol_ref, crow_ref, o_ref, *, tq, seq, lane0):
    i = pl.program_id(1)
    t_col = i * tq + lax.broadcasted_iota(I32, (tq, 1), 0)
    for span in range(CAUSAL_SPANS):
        klen = (span + 1) * (seq // CAUSAL_SPANS)

        @pl.when(_causal_span((i + 1) * tq, seq) == span)
        def _(klen=klen):
            causal = lax.broadcasted_iota(I32, (tq, klen), 1) <= t_col
            for h in range(FOX_HEADS):
                hs = slice(h * HEAD_DIM, (h + 1) * HEAD_DIM)
                s = _dot_nt(q_ref[:, hs], k_ref[0:klen, hs])
                s = s + ccol_ref[:, lane0 + h:lane0 + h + 1] - crow_ref[h:h + 1, 0:klen]
                s = jnp.where(causal, s, NEG_INF)
                e = jnp.exp(s - jnp.max(s, axis=-1, keepdims=True))
                o = _dot(e.astype(BF16), v_ref[0:klen, hs]) * (1.0 / jnp.sum(e, axis=-1, keepdims=True))
                o_ref[:, hs] = o.astype(o_ref.dtype)


def _fox(plain_proj, ccol, crow, B, S, lane0, tq=256):
    pp = plain_proj.reshape(B, S, -1)
    fw = FOX_HEADS * HEAD_DIM
    return pl.pallas_call(
        functools.partial(_fox_kernel, tq=tq, seq=S, lane0=lane0),
        grid=(B, S // tq),
        in_specs=[
            pl.BlockSpec((None, tq, fw), lambda b, i: (b, i, 1)),
            pl.BlockSpec((None, S, fw), lambda b, i: (b, 0, 2)),
            pl.BlockSpec((None, S, fw), lambda b, i: (b, 0, 3)),
            pl.BlockSpec((None, tq, LANES), lambda b, i: (b, i, 0)),
            pl.BlockSpec((None, FOX_HEADS, S), lambda b, i: (b, 0, 0)),
        ],
        out_specs=pl.BlockSpec((None, tq, fw), lambda b, i: (b, i, 0)),
        out_shape=jax.ShapeDtypeStruct((B, S, fw), BF16),
        compiler_params=_cparams(("parallel", "parallel")),
        name="fox",
    )(pp, pp, pp, ccol, crow)


def _merge_kernel(on_ref, of_ref, wn_ref, wf_ref, ga_ref, gb_ref, o_ref):
    a = _dot(on_ref[...], wn_ref[...])
    b = _dot(of_ref[...], wf_ref[...])
    o_ref[...] = (ga_ref[...].astype(F32) * a + gb_ref[...].astype(F32) * b).astype(o_ref.dtype)


def _merge(o_nsa, o_fox, w_up_nsa, w_up_fox, gates, d_model, tm=1024, tn=1024):
    T, K = o_nsa.shape
    nb = d_model // tn
    return pl.pallas_call(
        _merge_kernel,
        grid=(T // tm, nb),
        in_specs=[
            pl.BlockSpec((tm, K), lambda i, j: (i, 0)),
            pl.BlockSpec((tm, K), lambda i, j: (i, 0)),
            pl.BlockSpec((K, tn), lambda i, j: (0, j)),
            pl.BlockSpec((K, tn), lambda i, j: (0, j)),
            pl.BlockSpec((tm, tn), lambda i, j: (i, j)),
            pl.BlockSpec((tm, tn), lambda i, j: (i, j + nb)),
        ],
        out_specs=pl.BlockSpec((tm, tn), lambda i, j: (i, j)),
        out_shape=jax.ShapeDtypeStruct((T, d_model), BF16),
        compiler_params=_cparams(("parallel", "parallel")),
        name="merge",
    )(o_nsa, o_fox, w_up_nsa, w_up_fox, gates, gates)


def _rope_tables(pos):
    inv_freq = ROPE_THETA ** (-jnp.arange(0, HEAD_DIM, 2, dtype=F32) / HEAD_DIM)
    ang = pos.astype(F32)[:, None] * inv_freq[None, :]
    c, s = jnp.cos(ang), jnp.sin(ang)
    return jnp.concatenate([c, c], axis=-1), jnp.concatenate([-s, s], axis=-1)


def _mixer(x2, B, S, norm_gain, w_in, nsa_gate_bias, fox_forget_bias, merge_gate_bias, k_cmp_pos, k_cmp_w1, k_cmp_w2,
           v_cmp_pos, v_cmp_w1, v_cmp_w2, w_up_nsa, w_up_fox, w_out):
    T, D = x2.shape
    qw = NSA_HEADS * HEAD_DIM
    gw = NSA_GROUPS * HEAD_DIM
    fw = FOX_HEADS * HEAD_DIM
    ngate = 3 * NSA_HEADS
    widths = (qw, gw, gw, gw, gw, gw, gw, ngate, fw, fw, fw, FOX_HEADS, D, D)
    offs = np.concatenate([[0], np.cumsum(widths)])
    (w_qn, w_kc, w_vc, w_ks, w_vs, w_kw, w_vw, w_gn, w_qf, w_kf, w_vf, w_ff, w_ga, w_gb) = [
        w_in[:, offs[j]:offs[j + 1]] for j in range(len(widths))]
    w_rope = jnp.concatenate([w_qn, w_ks, w_kw], axis=1).astype(BF16)
    w_plain = jnp.concatenate([w_kc, w_vc, w_vs, w_vw, w_qf, w_kf, w_vf], axis=1).astype(BF16)
    w_gate = jnp.concatenate([w_ga, w_gb], axis=1).astype(BF16)
    pad = LANES - ngate - FOX_HEADS
    w_small = jnp.concatenate([w_gn, w_ff, jnp.zeros((D, pad), w_in.dtype)], axis=1).astype(BF16)
    gate_bias = jnp.concatenate([nsa_gate_bias, jnp.zeros((LANES - ngate,), F32)]).reshape(1, LANES)
    fbias = jnp.concatenate([jnp.zeros((ngate,), F32), fox_forget_bias, jnp.zeros((pad,), F32)]).reshape(1, LANES)

    (h,) = _rmsnorm(x2, norm_gain, (BF16,))
    cos_t, sin_t = _rope_tables(jnp.arange(S))
    rope_proj = _matmul(h, w_rope, BF16, "rope", (cos_t, sin_t), seq=S, scaled_cols=(0, qw), name="inproj_rope")
    plain_proj = _matmul(h, w_plain, BF16, scaled_cols=(4 * gw, 4 * gw + fw), tn=1024, name="inproj_plain")
    gates = _matmul(h, w_gate, BF16, "sigmoid_bias", (merge_gate_bias.reshape(1, 2 * D),), tn=1024, name="inproj_gate")
    small_proj = _matmul(h, w_small, F32, name="inproj_small")

    n_chunks = S // CMP_STRIDE
    ck = CMP_STRIDE * HEAD_DIM

    def chunks(col0):
        c = plain_proj[:, col0:col0 + gw].reshape(B, S, NSA_GROUPS, HEAD_DIM).transpose(0, 2, 1, 3)
        return c.reshape(B * NSA_GROUPS, n_chunks, ck)

    cos_c, sin_c = _rope_tables(jnp.arange(n_chunks) * CMP_STRIDE + CMP_BLOCK - 1)
    kc = _compress(chunks(0), k_cmp_pos, k_cmp_w1, k_cmp_w2, cos_c, sin_c, True)
    vc = _compress(chunks(gw), v_cmp_pos, v_cmp_w1, v_cmp_w2, cos_c, sin_c, False)
    kc = kc.reshape(B, NSA_GROUPS, n_chunks, HEAD_DIM)
    vc = vc.reshape(B, NSA_GROUPS, n_chunks, HEAD_DIM)

    o_nsa = _nsa(rope_proj, plain_proj, kc, vc, small_proj, gate_bias, B, S)
    ccol, crow = _forget_cumsum(small_proj, fbias, B, S, ngate)
    o_fox = _fox(plain_proj, ccol, crow, B, S, ngate)

    merged = _merge(o_nsa.reshape(T, qw), o_fox.reshape(T, fw), w_up_nsa.astype(BF16), w_up_fox.astype(BF16), gates, D)
    return _matmul(merged, w_out.astype(BF16), F32, "residual", (x2,), tn=1024, name="outproj")


def _candidate_pieces():
    pieces = [((0, 1), (0, PEER_TOPK))]
    pieces += [((k, k + 1), (0, SUBLANES)) for k in range(1, SUBLANES)]
    pieces += [((SUBLANES, PEER_TOPK), (0, 1))]
    covered = {(a, b) for (a0, a1), (b0, b1) in pieces for a in range(a0, a1) for b in range(b0, b1)}
    assert all((a, b) in covered for a in range(PEER_TOPK) for b in range(PEER_TOPK) if (a + 1) * (b + 1) <= PEER_TOPK)
    return pieces


def _candidate_flat_index(tm):
    rows = [a * PEER_TOPK + b for (a0, a1), (b0, b1) in _candidate_pieces() for a in range(a0, a1) for b in range(b0, b1)]
    return jnp.asarray(np.broadcast_to(np.asarray(rows, np.float32)[:, None], (len(rows), tm)))


def _route_kernel(xn_ref, wq_ref, k1_ref, k2_ref, flat_ref, e_ref, g_ref, *, tm):
    q = _dot(xn_ref[...], wq_ref[...])
    s1 = _dot_nt(k1_ref[...], q[:, :PEER_HALF].astype(BF16))
    s2 = _dot_nt(k2_ref[...], q[:, PEER_HALF:].astype(BF16))
    key_row = lax.broadcasted_iota(I32, (PEER_N_KEYS, tm), 0).astype(F32)
    rank_row = lax.broadcasted_iota(I32, (PEER_TOPK, tm), 0)
    cand_row = flat_ref[...]
    nc = float(PEER_TOPK * PEER_TOPK)

    def top_keys(s):
        vals = jnp.zeros((PEER_TOPK, tm), F32)
        idxs = jnp.zeros((PEER_TOPK, tm), F32)
        for r in range(PEER_TOPK):
            m = jnp.max(s, axis=0, keepdims=True)
            idx = jnp.min(jnp.where(s == m, key_row, float(PEER_N_KEYS)), axis=0, keepdims=True)
            vals = jnp.where(rank_row == r, m, vals)
            idxs = jnp.where(rank_row == r, idx, idxs)
            s = jnp.where(key_row == idx, -jnp.inf, s)
        return vals, idxs

    v1, i1 = top_keys(s1)
    v2, i2 = top_keys(s2)
    pieces = _candidate_pieces()
    cand = jnp.concatenate([v1[a0:a1, :] + v2[b0:b1, :] for (a0, a1), (b0, b1) in pieces], axis=0)
    cidx = jnp.concatenate([i1[a0:a1, :] * float(PEER_N_KEYS) + i2[b0:b1, :] for (a0, a1), (b0, b1) in pieces],
                           axis=0)

    top_s = jnp.zeros((PEER_TOPK, tm), F32)
    top_e = jnp.zeros((PEER_TOPK, tm), F32)
    for r in range(PEER_TOPK):
        m = jnp.max(cand, axis=0, keepdims=True)
        pos = jnp.min(jnp.where(cand == m, cand_row, nc), axis=0, keepdims=True)
        hit = cand_row == pos
        e = jnp.sum(jnp.where(hit, cidx, 0.0), axis=0, keepdims=True)
        top_s = jnp.where(rank_row == r, m, top_s)
        top_e = jnp.where(rank_row == r, e, top_e)
        cand = jnp.where(hit, -jnp.inf, cand)

    ex = jnp.exp(top_s - top_s[0:1, :])
    e_ref[...] = top_e.astype(I32)
    g_ref[...] = ex / jnp.sum(ex, axis=0, keepdims=True)


def _route(xn_bf16, w_query, sub_keys_1, sub_keys_2, tm=1024):
    T, D = xn_bf16.shape
    kd = 2 * PEER_HALF
    flat = _candidate_flat_index(tm)
    return pl.pallas_call(
        functools.partial(_route_kernel, tm=tm),
        grid=(T // tm, PEER_HEADS),
        in_specs=[
            pl.BlockSpec((tm, D), lambda i, h: (i, 0)),
            pl.BlockSpec((D, kd), lambda i, h: (0, h)),
            pl.BlockSpec((None, PEER_N_KEYS, PEER_HALF), lambda i, h: (h, 0, 0)),
            pl.BlockSpec((None, PEER_N_KEYS, PEER_HALF), lambda i, h: (h, 0, 0)),
            pl.BlockSpec(flat.shape, lambda i, h: (0, 0)),
        ],
        out_specs=[pl.BlockSpec((PEER_TOPK, tm), lambda i, h: (h, i))] * 2,
        out_shape=[jax.ShapeDtypeStruct((PEER_SLOTS, T), I32), jax.ShapeDtypeStruct((PEER_SLOTS, T), F32)],
        compiler_params=_cparams(("parallel", "parallel")),
        name="peer_route",
    )(xn_bf16, w_query.astype(BF16), sub_keys_1.astype(BF16), sub_keys_2.astype(BF16), flat)


def _slot_sort_kernel(e_ref, g_ref, se_ref, sg_ref, meta_ref, *, tm, part_shift):
    row = lax.broadcasted_iota(I32, (PEER_SLOTS, tm), 0)
    e = e_ref[...]
    g = g_ref[...]
    key = e * PEER_SLOTS + row
    rank = jnp.zeros((PEER_SLOTS, tm), I32)
    for k in range(PEER_SLOTS):
        rank = rank + (key[k:k + 1, :] < key).astype(I32)
    se = jnp.zeros((PEER_SLOTS, tm), I32)
    sg = jnp.zeros((PEER_SLOTS, tm), F32)
    for k in range(PEER_SLOTS):
        hit = rank[k:k + 1, :] == row
        se = jnp.where(hit, e[k:k + 1, :], se)
        sg = jnp.where(hit, g[k:k + 1, :], sg)
    part = se >> part_shift
    meta = jnp.zeros((PEER_SLOTS, tm), I32)
    start = jnp.zeros((1, tm), I32)
    for p in range(PEER_PARTS):
        cnt = jnp.sum((part == p).astype(I32), axis=0, keepdims=True)
        meta = jnp.where(row == p, start, meta)
        meta = jnp.where(row == PEER_PARTS + p, cnt, meta)
        start = start + cnt
    se_ref[...] = (se & ((1 << part_shift) - 1)).T
    sg_ref[...] = sg.T
    meta_ref[...] = meta.T


def _slot_sort(e, g, part_shift, tm=256):
    T = e.shape[1]
    in_spec = pl.BlockSpec((PEER_SLOTS, tm), lambda i: (0, i))
    out_spec = pl.BlockSpec((tm, PEER_SLOTS), lambda i: (i, 0))
    return pl.pallas_call(
        functools.partial(_slot_sort_kernel, tm=tm, part_shift=part_shift),
        grid=(T // tm,),
        in_specs=[in_spec, in_spec],
        out_specs=[out_spec, out_spec, out_spec],
        out_shape=[jax.ShapeDtypeStruct((T, PEER_SLOTS), I32), jax.ShapeDtypeStruct((T, PEER_SLOTS), F32),
                   jax.ShapeDtypeStruct((T, PEER_SLOTS), I32)],
        compiler_params=_cparams(("parallel",)),
        name="peer_slot_sort",
    )(e, g)


SUBLANES = 8
META_W = 2 * PEER_PARTS
_BITREV3 = (0, 4, 2, 6, 1, 5, 3, 7)


FINISH_BATCH = 64
DOWN_GROUP = 32
UP_GROUP = 32


def _slot_groups(meta_ref, t, p, group):
    shift = group.bit_length() - 1
    start = meta_ref[t * META_W + p]
    cnt = meta_ref[t * META_W + PEER_PARTS + p]
    g0 = lax.shift_right_logical(start, shift)
    g1 = jnp.where(cnt > 0, lax.shift_right_logical(start + cnt + (group - 1), shift), g0)
    return g0, g1


def _fold_rows(a, b, first, sh):
    return jnp.where(first, a, b) + pltpu.roll(jnp.where(first, b, a), sh, axis=0)


def _fold_masks():
    sub = lax.broadcasted_iota(I32, (SUBLANES, LANES), 0)
    band = lambda lo: (sub >= lo) & (sub < lo + 4)
    return (band(0), band(2), band(1), band(3)), ((sub & 2) == 0, ((sub + 7) & 2) == 0), (sub & 1) == 0


def _sublane_totals(parts, masks):
    quarter, half, even = masks
    q = [_fold_rows(parts[2 * m], parts[2 * m + 1], quarter[m], 4) for m in range(4)]
    h = [_fold_rows(q[0], q[1], half[0], 6), _fold_rows(q[2], q[3], half[1], 6)]
    return _fold_rows(h[0], h[1], even, 7)


def _peer_down_kernel(u_ref, x_ref, se_ref, meta_ref, sg_ref, w_ref, r_scr, a_scr, *, tb):
    p = pl.program_id(0)
    lane1 = lax.broadcasted_iota(I32, (1, LANES), 1)

    @pl.when((p == 0) & (pl.program_id(1) == 0))
    def _():
        r_scr[...] = jnp.zeros_like(r_scr)

    def token(t, carry):
        g0, g1 = _slot_groups(meta_ref, t, p, DOWN_GROUP)
        xt = x_ref[t]

        def group(g, c2):
            base = pl.multiple_of(t * PEER_SLOTS + g * DOWN_GROUP, DOWN_GROUP)
            for tile in range(0, DOWN_GROUP, SUBLANES):
                parts = []
                for k in range(SUBLANES):
                    pr = xt * u_ref[se_ref[base + tile + _BITREV3[k]]].astype(F32)
                    parts.append(pr[0:8] + pr[8:16])
                r_scr[pl.ds(base + tile, SUBLANES), :] = _sublane_totals(parts, _fold_masks())
            return c2

        lax.fori_loop(g0, g1, group, 0)
        return carry

    lax.fori_loop(0, tb, token, 0)

    def finish(b, carry):
        for tl in range(FINISH_BATCH):
            t = b * FINISH_BATCH + tl
            start = meta_ref[t * META_W + p]
            cnt = meta_ref[t * META_W + PEER_PARTS + p]
            r_t = r_scr[pl.ds(pl.multiple_of(t * PEER_SLOTS, PEER_SLOTS), PEER_SLOTS), :]
            a_row = jnp.sum(r_t.T, axis=0, keepdims=True)
            valid = (lane1 >= start) & (lane1 < start + cnt)
            a_scr[pl.ds(t, 1), :] = jnp.where(valid, a_row, 0.0)
        return carry

    lax.fori_loop(0, tb // FINISH_BATCH, finish, 0)
    w_ref[...] = jax.nn.gelu(a_scr[...]) * sg_ref[...]


def _peer_up_kernel(v_ref, y_ref, se_ref, meta_ref, w_ref, o_ref, *, tb):
    p = pl.program_id(0)

    def token(t, carry):
        g0, g1 = _slot_groups(meta_ref, t, p, UP_GROUP)

        def group(g, acc):
            base = t * PEER_SLOTS + g * UP_GROUP
            for i in range(UP_GROUP):
                acc = acc + w_ref[base + i] * v_ref[se_ref[base + i]].astype(F32)
            return acc

        o_ref[t] = lax.fori_loop(g0, g1, group, y_ref[t])
        return carry

    lax.fori_loop(0, tb, token, 0)


def _peer_experts(xn3, resid3, se, sg, meta, down_tab, up_tab, tb=128):
    T = xn3.shape[0]
    E = down_tab.shape[0]
    part_size = E // PEER_PARTS
    nt = T // tb
    rows = xn3.shape[1]
    se_flat = se.reshape(T * PEER_SLOTS)
    meta_flat = meta[:, :META_W].reshape(T * META_W)
    tab_spec = pl.BlockSpec((part_size, rows, LANES), lambda p, i: (p, 0, 0), pipeline_mode=pl.Buffered(1))
    tok_spec = pl.BlockSpec((tb, rows, LANES), lambda p, i: (i, 0, 0))
    se_spec = pl.BlockSpec((tb * PEER_SLOTS,), lambda p, i: (i,), memory_space=pltpu.SMEM)
    meta_spec = pl.BlockSpec((tb * META_W,), lambda p, i: (i,), memory_space=pltpu.SMEM)
    w_parts = pl.pallas_call(
        functools.partial(_peer_down_kernel, tb=tb),
        grid=(PEER_PARTS, nt),
        in_specs=[tab_spec, tok_spec, se_spec, meta_spec, pl.BlockSpec((tb, PEER_SLOTS), lambda p, i: (i, 0))],
        out_specs=pl.BlockSpec((None, tb, PEER_SLOTS), lambda p, i: (p, i, 0)),
        out_shape=jax.ShapeDtypeStruct((PEER_PARTS, T, PEER_SLOTS), F32),
        scratch_shapes=[pltpu.VMEM((tb * PEER_SLOTS, LANES), F32), pltpu.VMEM((tb, PEER_SLOTS), F32)],
        compiler_params=_cparams(("arbitrary", "arbitrary")),
        name="peer_down",
    )(down_tab, xn3, se_flat, meta_flat, sg)
    w_flat = w_parts.reshape(PEER_PARTS * T * PEER_SLOTS)
    w_spec = pl.BlockSpec((tb * PEER_SLOTS,), lambda p, i: (p * nt + i,), memory_space=pltpu.SMEM)
    return pl.pallas_call(
        functools.partial(_peer_up_kernel, tb=tb),
        grid=(PEER_PARTS, nt),
        in_specs=[tab_spec, tok_spec, se_spec, meta_spec, w_spec],
        out_specs=tok_spec,
        out_shape=jax.ShapeDtypeStruct(resid3.shape, F32),
        input_output_aliases={1: 0},
        compiler_params=_cparams(("arbitrary", "arbitrary")),
        name="peer_up",
    )(up_tab, resid3, se_flat, meta_flat, w_flat)


def _peer(x2, norm_gain, w_query, sub_keys_1, sub_keys_2, expert_down, expert_up):
    T, D = x2.shape
    E = expert_down.shape[0]
    rows = D // LANES
    xn_bf16, xn = _rmsnorm(x2, norm_gain, (BF16, F32))
    e, g = _route(xn_bf16, w_query, sub_keys_1, sub_keys_2)
    part_shift = int(math.log2(E // PEER_PARTS))
    se, sg, meta = _slot_sort(e, g, part_shift)
    down_tab = expert_down.astype(BF16).reshape(E, rows, LANES)
    up_tab = expert_up.astype(BF16).reshape(E, rows, LANES)
    y = _peer_experts(xn.reshape(T, rows, LANES), x2.reshape(T, rows, LANES), se, sg, meta, down_tab, up_tab)
    return y.reshape(T, D)


def kernel(x, norm_mix_gain, w_in, nsa_gate_bias, fox_forget_bias, merge_gate_bias, k_cmp_pos, k_cmp_w1, k_cmp_w2, v_cmp_pos, v_cmp_w1, v_cmp_w2, w_up_nsa, w_up_fox, w_out, norm_ffn_gain, peer_w_query, peer_sub_keys_1, peer_sub_keys_2, peer_expert_down, peer_expert_up, norm_final_gain):
    B, S, D = x.shape
    x2 = x.reshape(B * S, D)
    for l in range(norm_mix_gain.shape[0]):
        x2 = _mixer(x2, B, S, norm_mix_gain[l], w_in[l], nsa_gate_bias[l], fox_forget_bias[l], merge_gate_bias[l],
                    k_cmp_pos[l], k_cmp_w1[l], k_cmp_w2[l], v_cmp_pos[l], v_cmp_w1[l], v_cmp_w2[l],
                    w_up_nsa[l], w_up_fox[l], w_out[l])
        x2 = _peer(x2, norm_ffn_gain[l], peer_w_query[l], peer_sub_keys_1[l], peer_sub_keys_2[l], peer_expert_down[l],
                   peer_expert_up[l])
    (out,) = _rmsnorm(x2, norm_final_gain, (F32,))
    return out.reshape(B, S, D)
```
